```python
import jax, jax.numpy as jnp
from jax import lax
import numpy as np

D_MODEL = 1024
BATCH = 2
SEQ = 8192
DEPTH = 1

MEM_LEN = 256
FOX_HEADS = 8
FOX_HEAD_DIM = 64
DSA_HEADS = 8
DSA_HEAD_DIM = 64
IDX_HEADS = 8
IDX_DIM = 64
MEM_HEADS = 4
MEM_HEAD_DIM = 128
BRANCH_WIDTH = 512
N_BRANCHES = 3
D_FF = 2816
ROPE_THETA = 10000.0
Q_BLOCK = 128
TOPK_MAX = 256
EPS = 1e-6

SPLIT_SIZES = (
    BRANCH_WIDTH, BRANCH_WIDTH, BRANCH_WIDTH,
    FOX_HEADS,
    BRANCH_WIDTH, DSA_HEAD_DIM, DSA_HEAD_DIM,
    IDX_HEADS * IDX_DIM, IDX_DIM, IDX_HEADS,
    BRANCH_WIDTH,
    N_BRANCHES * D_MODEL,
)
SPLIT_POINTS = tuple(int(v) for v in np.cumsum(SPLIT_SIZES)[:-1])
D_IN = int(sum(SPLIT_SIZES))

kernel_name = "hybrid_fox_dsa_mem_macaron"


def rmsnorm(x, g):
    xf = x.astype(jnp.float32)
    y = xf * lax.rsqrt(jnp.mean(xf * xf, axis=-1, keepdims=True) + EPS)
    return (y * g.astype(jnp.float32)).astype(x.dtype)


def swiglu(x, w_gate, w_up, w_down):
    return (jax.nn.silu(x @ w_gate) * (x @ w_up)) @ w_down


def rope_tables(positions, dim, dtype):
    inv_freq = ROPE_THETA ** (-jnp.arange(0, dim, 2, dtype=jnp.float32) / dim)
    ang = positions.astype(jnp.float32)[..., None] * inv_freq
    return jnp.cos(ang)[:, :, None, :].astype(dtype), jnp.sin(ang)[:, :, None, :].astype(dtype)


def rope(x, cos, sin):
    x1, x2 = jnp.split(x, 2, axis=-1)
    return jnp.concatenate([x1 * cos - x2 * sin, x2 * cos + x1 * sin], axis=-1)


def to_blocks(a):
    B, S = a.shape[:2]
    return a.reshape((B, S // Q_BLOCK, Q_BLOCK) + a.shape[2:]).swapaxes(0, 1)


def from_blocks(a):
    nb, B, Q = a.shape[:3]
    return a.swapaxes(0, 1).reshape((B, nb * Q) + a.shape[3:])


def fox_attention(q, k, v, logf):
    B, S, H, hd = q.shape
    c = jnp.cumsum(logf, axis=1)
    c_keys = c.transpose(0, 2, 1)
    key_pos = jnp.arange(S)
    scale = hd ** -0.5

    def block(args):
        qi, ci, bi = args
        qpos = bi * Q_BLOCK + jnp.arange(Q_BLOCK)
        s = jnp.einsum('bqhd,bkhd->bhqk', qi, k).astype(jnp.float32) * scale
        s = s + ci.transpose(0, 2, 1)[..., None] - c_keys[:, :, None, :]
        s = jnp.where(key_pos[None, :] <= qpos[:, None], s, -jnp.inf)
        p = jax.nn.softmax(s, axis=-1)
        return jnp.einsum('bhqk,bkhd->bqhd', p.astype(v.dtype), v)

    out = lax.map(block, (to_blocks(q), to_blocks(c), jnp.arange(S // Q_BLOCK)))
    return from_blocks(out).reshape(B, S, H * hd)


def dsa_attention(q, k, v, iq, ik, iw, k_sel):
    B, S, H, hd = q.shape
    key_pos = jnp.arange(S)
    scale = hd ** -0.5
    gather = jax.vmap(lambda a, i: a[i])

    def block(args):
        qi, iqi, iwi, bi = args
        qpos = bi * Q_BLOCK + jnp.arange(Q_BLOCK)
        sc = jnp.einsum('bqhd,bkd->bqhk', iqi, ik).astype(jnp.float32)
        score = jnp.einsum('bqh,bqhk->bqk', iwi.astype(jnp.float32), jax.nn.relu(sc))
        score = jnp.where((key_pos[None, :] <= qpos[:, None])[None], score, -jnp.inf)
        _, idx = lax.top_k(score, k_sel)
        valid = idx <= qpos[None, :, None]
        ks = gather(k, idx)
        vs = gather(v, idx)
        s = jnp.einsum('bqhd,bqkd->bqhk', qi, ks).astype(jnp.float32) * scale
        s = jnp.where(valid[:, :, None, :], s, -jnp.inf)
        p = jax.nn.softmax(s, axis=-1)
        return jnp.einsum('bqhk,bqkd->bqhd', p.astype(vs.dtype), vs)

    out = lax.map(block, (to_blocks(q), to_blocks(iq), to_blocks(iw), jnp.arange(S // Q_BLOCK)))
    return from_blocks(out).reshape(B, S, H * hd)


def memory_attention(q, km, vm):
    B, S, H, hd = q.shape
    s = jnp.einsum('bshd,bmhd->bhsm', q, km).astype(jnp.float32) * hd ** -0.5
    p = jax.nn.softmax(s, axis=-1)
    return jnp.einsum('bhsm,bmhd->bshd', p.astype(vm.dtype), vm).reshape(B, S, H * hd)


def setup_inputs(seed: int = 0) -> dict:
    key = jax.random.key(seed)
    ks = jax.random.split(key, 24)
    f32 = jnp.float32

    def w(k, shape, fan_in):
        return jax.random.normal(k, shape, f32) * fan_in ** -0.5

    def gain(k, shape):
        return 1.0 + 0.02 * jax.random.normal(k, shape, f32)

    L = DEPTH
    return {
        "x": jax.random.normal(ks[0], (BATCH, SEQ, D_MODEL), f32),
        "mem": jax.random.normal(ks[1], (BATCH, MEM_LEN, D_MODEL), f32),
        "positions": jnp.broadcast_to(jnp.arange(SEQ, dtype=jnp.int32), (BATCH, SEQ)),
        "ffn1_norm": gain(ks[2], (L, D_MODEL)),
        "ffn1_w_gate": w(ks[3], (L, D_MODEL, D_FF), D_MODEL),
        "ffn1_w_up": w(ks[4], (L, D_MODEL, D_FF), D_MODEL),
        "ffn1_w_down": w(ks[5], (L, D_FF, D_MODEL), D_FF),
        "mix_norm": gain(ks[6], (L, D_MODEL)),
        "mem_norm": gain(ks[7], (L, D_MODEL)),
        "w_in": w(ks[8], (L, D_MODEL, D_IN), D_MODEL),
        "b_forget": 3.0 + 0.5 * jax.random.normal(ks[9], (L, FOX_HEADS), f32),
        "fox_q_norm": gain(ks[10], (L, FOX_HEAD_DIM)),
        "fox_k_norm": gain(ks[11], (L, FOX_HEAD_DIM)),
        "dsa_q_norm": gain(ks[12], (L, DSA_HEAD_DIM)),
        "dsa_k_norm": gain(ks[13], (L, DSA_HEAD_DIM)),
        "mem_q_norm": gain(ks[14], (L, MEM_HEAD_DIM)),
        "mem_k_norm": gain(ks[15], (L, MEM_HEAD_DIM)),
        "w_mem_kv": w(ks[16], (L, D_MODEL, 2 * BRANCH_WIDTH), D_MODEL),
        "w_branch": w(ks[17], (L, N_BRANCHES, BRANCH_WIDTH, D_MODEL), BRANCH_WIDTH),
        "w_out": w(ks[18], (L, D_MODEL, D_MODEL), D_MODEL),
        "ffn2_norm": gain(ks[19], (L, D_MODEL)),
        "ffn2_w_gate": w(ks[20], (L, D_MODEL, D_FF), D_MODEL),
        "ffn2_w_up": w(ks[21], (L, D_MODEL, D_FF), D_MODEL),
        "ffn2_w_down": w(ks[22], (L, D_FF, D_MODEL), D_FF),
    }


def reference(x, mem, positions, ffn1_norm, ffn1_w_gate, ffn1_w_up, ffn1_w_down, mix_norm,
              mem_norm, w_in, b_forget, fox_q_norm, fox_k_norm, dsa_q_norm, dsa_k_norm,
              mem_q_norm, mem_k_norm, w_mem_kv, w_branch, w_out, ffn2_norm, ffn2_w_gate,
              ffn2_w_up, ffn2_w_down):
    B, S, D = x.shape
    M = mem.shape[1]
    k_sel = min(TOPK_MAX, S // 4)
    cos, sin = rope_tables(positions, DSA_HEAD_DIM, x.dtype)
    cos_i, sin_i = rope_tables(positions, IDX_DIM, x.dtype)
    h = x
    for l in range(DEPTH):
        h = h + 0.5 * swiglu(rmsnorm(h, ffn1_norm[l]), ffn1_w_gate[l], ffn1_w_up[l], ffn1_w_down[l])

        u = rmsnorm(h, mix_norm[l])
        (fq, fk, fv, ff, dq, dk, dv, iq, ik, iw, mq, g) = jnp.split(u @ w_in[l], SPLIT_POINTS, axis=-1)

        fq = rmsnorm(fq.reshape(B, S, FOX_HEADS, FOX_HEAD_DIM), fox_q_norm[l])
        fk = rmsnorm(fk.reshape(B, S, FOX_HEADS, FOX_HEAD_DIM), fox_k_norm[l])
        fv = fv.reshape(B, S, FOX_HEADS, FOX_HEAD_DIM)
        logf = jax.nn.log_sigmoid(ff.astype(jnp.float32) + b_forget[l].astype(jnp.float32))
        o_a = fox_attention(fq, fk, fv, logf)

        dq = rope(rmsnorm(dq.reshape(B, S, DSA_HEADS, DSA_HEAD_DIM), dsa_q_norm[l]), cos, sin)
        dk = rope(rmsnorm(dk[:, :, None, :], dsa_k_norm[l]), cos, sin)[:, :, 0, :]
        iq = rope(iq.reshape(B, S, IDX_HEADS, IDX_DIM), cos_i, sin_i)
        ik = rope(ik[:, :, None, :], cos_i, sin_i)[:, :, 0, :]
        iw = iw * (IDX_HEADS ** -0.5 * IDX_DIM ** -0.5)
        o_b = dsa_attention(dq, dk, dv, iq, ik, iw, k_sel)

        km, vm = jnp.split(rmsnorm(mem, mem_norm[l]) @ w_mem_kv[l], 2, axis=-1)
        km = rmsnorm(km.reshape(B, M, MEM_HEADS, MEM_HEAD_DIM), mem_k_norm[l])
        vm = vm.reshape(B, M, MEM_HEADS, MEM_HEAD_DIM)
        mq = rmsnorm(mq.reshape(B, S, MEM_HEADS, MEM_HEAD_DIM), mem_q_norm[l])
        o_c = memory_attention(mq, km, vm)

        branches = jnp.stack([o_a, o_b, o_c], axis=2)
        proj = jnp.einsum('bsnw,nwd->bsnd', branches, w_branch[l])
        gates = jax.nn.sigmoid(g.reshape(B, S, N_BRANCHES, D))
        h = h + jnp.sum(gates * proj, axis=2) @ w_out[l]

        h = h + 0.5 * swiglu(rmsnorm(h, ffn2_norm[l]), ffn2_w_gate[l], ffn2_w_up[l], ffn2_w_down[l])
    return h
```

```python
import functools

import jax
import jax.numpy as jnp
import numpy as np
from jax import lax
from jax.experimental import pallas as pl
from jax.experimental.pallas import tpu as pltpu

F32 = jnp.float32
BF16 = jnp.bfloat16
I32 = jnp.int32

D_MODEL = 1024
N_HEADS = 8
HEAD_DIM = 64
MEM_HEADS = 4
MEM_HEAD_DIM = 128
BRANCH_WIDTH = 512
N_BRANCHES = 3
ROPE_THETA = 10000.0
TOPK_MAX = 256
EPS = 1e-6

LANES = 128
VMEM_LIMIT_BYTES = 56 * 1024 * 1024
INT_MIN = -(2 ** 31)
NEG_BIG = -1e30


def _params(*sem):
    return pltpu.CompilerParams(dimension_semantics=sem, vmem_limit_bytes=VMEM_LIMIT_BYTES)


def _rms(x, g):
    return x * lax.rsqrt(jnp.mean(x * x, axis=-1, keepdims=True) + EPS) * g


def _head_rms(x, g):
    ms = jnp.sum(x * x, axis=-1, keepdims=True) * (1.0 / HEAD_DIM)
    return x * lax.rsqrt(ms + EPS) * g


def _dot(a, b):
    return jnp.dot(a, b, preferred_element_type=F32)


def _dot_nt(a, b):
    return lax.dot_general(a, b, (((1,), (1,)), ((), ())), preferred_element_type=F32)


def _split3(x):
    a = x.astype(BF16)
    r = x - a.astype(F32)
    b = r.astype(BF16)
    c = (r - b.astype(F32)).astype(BF16)
    return a, b, c


def _ffn_kernel(x_ref, g_ref, wg_ref, wu_ref, wd_ref, o_ref, xn_ref, acc_ref):
    f = pl.program_id(1)

    @pl.when(f == 0)
    def _():
        xn_ref[...] = _rms(x_ref[...], g_ref[...]).astype(BF16)
        acc_ref[...] = jnp.zeros_like(acc_ref)

    xn = xn_ref[...]
    gate = _dot(xn, wg_ref[...])
    up = _dot(xn, wu_ref[...])
    act = gate * jax.nn.sigmoid(gate) * up
    acc_ref[...] += _dot(act.astype(BF16), wd_ref[...])

    @pl.when(f == pl.num_programs(1) - 1)
    def _():
        o_ref[...] = x_ref[...] + 0.5 * acc_ref[...]


def _ffn(x, g, wg, wu, wd, *, tm, tf):
    n, d = x.shape
    d_ff = wg.shape[1]
    return pl.pallas_call(
        _ffn_kernel,
        grid=(n // tm, d_ff // tf),
        in_specs=[
            pl.BlockSpec((tm, d), lambda i, f: (i, 0)),
            pl.BlockSpec((1, d), lambda i, f: (0, 0)),
            pl.BlockSpec((d, tf), lambda i, f: (0, f)),
            pl.BlockSpec((d, tf), lambda i, f: (0, f)),
            pl.BlockSpec((tf, d), lambda i, f: (f, 0)),
        ],
        out_specs=pl.BlockSpec((tm, d), lambda i, f: (i, 0)),
        out_shape=jax.ShapeDtypeStruct((n, d), F32),
        scratch_shapes=[pltpu.VMEM((tm, d), BF16), pltpu.VMEM((tm, d), F32)],
        compiler_params=_params("parallel", "arbitrary"),
        name="ffn",
    )(x, g, wg, wu, wd)


def _proj_fox_kernel(h_ref, gmix_ref, w_ref, bf_ref, gq_ref, gk_ref, q_ref, k_ref, v_ref, carry_ref, *, tm):
    @pl.when(pl.program_id(1) == 0)
    def _():
        carry_ref[...] = jnp.zeros_like(carry_ref)

    u = _rms(h_ref[...], gmix_ref[...]).astype(BF16)
    p = _dot(u, w_ref[...])
    qw = N_HEADS * LANES
    v_ref[...] = p[:, 2 * qw:2 * qw + BRANCH_WIDTH].astype(BF16)

    lane = lax.broadcasted_iota(I32, (tm, LANES), 1)
    z = p[:, 2 * qw + BRANCH_WIDTH:] + bf_ref[...]
    logf = -(jnp.maximum(-z, 0.0) + jnp.log1p(jnp.exp(-jnp.abs(z))))
    logf = jnp.where(lane < N_HEADS, logf, 0.0)

    row = lax.broadcasted_iota(I32, (tm, tm), 0)
    col = lax.broadcasted_iota(I32, (tm, tm), 1)
    tri = jnp.where(col <= row, 1.0, 0.0).astype(BF16)
    l1, l2, l3 = _split3(logf)
    c = _dot(tri, l1) + _dot(tri, l2) + _dot(tri, l3) + carry_ref[...]
    carry_ref[...] = c[tm - 1:tm, :]

    gq = gq_ref[...]
    gk = gk_ref[...]
    for h in range(N_HEADS):
        qn = _head_rms(p[:, h * LANES:(h + 1) * LANES], gq) * (HEAD_DIM ** -0.5)
        kn = _head_rms(p[:, qw + h * LANES:qw + (h + 1) * LANES], gk)
        c1, c2, c3 = (t.astype(F32) for t in _split3(c[:, h:h + 1]))
        qa = jnp.where(lane < 64, qn, jnp.where(lane == 64, c1, jnp.where(lane == 65, c2, jnp.where(
            lane == 66, c3, jnp.where(lane < 70, 1.0, 0.0)))))
        ka = jnp.where(lane < 64, kn, jnp.where(lane < 67, 1.0, jnp.where(lane == 67, -c1, jnp.where(
            lane == 68, -c2, jnp.where(lane == 69, -c3, 0.0)))))
        q_ref[:, h * LANES:(h + 1) * LANES] = qa.astype(BF16)
        k_ref[:, h * LANES:(h + 1) * LANES] = ka.astype(BF16)


def _proj_fox(h, gmix, w, bf, gq, gk, *, tm):
    b, s, d = h.shape
    qw = N_HEADS * LANES
    nw = w.shape[1]
    const = lambda bi, i: (0, 0)
    return pl.pallas_call(
        functools.partial(_proj_fox_kernel, tm=tm),
        grid=(b, s // tm),
        in_specs=[
            pl.BlockSpec((None, tm, d), lambda bi, i: (bi, i, 0)),
            pl.BlockSpec((1, d), const),
            pl.BlockSpec((d, nw), const),
            pl.BlockSpec((1, LANES), const),
            pl.BlockSpec((1, LANES), const),
            pl.BlockSpec((1, LANES), const),
        ],
        out_specs=[
            pl.BlockSpec((None, tm, qw), lambda bi, i: (bi, i, 0)),
            pl.BlockSpec((None, tm, qw), lambda bi, i: (bi, i, 0)),
            pl.BlockSpec((None, tm, BRANCH_WIDTH), lambda bi, i: (bi, i, 0)),
        ],
        out_shape=[
            jax.ShapeDtypeStruct((b, s, qw), BF16),
            jax.ShapeDtypeStruct((b, s, qw), BF16),
            jax.ShapeDtypeStruct((b, s, BRANCH_WIDTH), BF16),
        ],
        scratch_shapes=[pltpu.VMEM((1, LANES), F32)],
        compiler_params=_params("parallel", "arbitrary"),
        name="proj_fox",
    )(h, gmix, w, bf, gq, gk)


def _fox_attn_kernel(q_ref, k_ref, v_ref, o_ref, *, t):
    i = pl.program_id(2)
    lane = lax.broadcasted_iota(I32, (t, LANES), 1)
    row = lax.broadcasted_iota(I32, (t, t), 0)
    col = lax.broadcasted_iota(I32, (t, t), 1)
    outs = []
    for hh in range(2):
        q = q_ref[:, hh * LANES:(hh + 1) * LANES]

        def step(j, carry, masked):
            m, l, acc = carry
            start = pl.multiple_of(j * t, t)
            s = _dot_nt(q, k_ref[pl.ds(start, t), hh * LANES:(hh + 1) * LANES])
            if masked:
                s = jnp.where(col <= row, s, -jnp.inf)
            m_new = jnp.maximum(m, jnp.max(s, axis=-1, keepdims=True))
            p = jnp.exp(s - m_new)
            alpha = jnp.exp(m - m_new)
            l = alpha * l + jnp.sum(p, axis=-1, keepdims=True)
            acc = alpha * acc + _dot(p.astype(BF16), v_ref[pl.ds(start, t), :])
            return m_new, l, acc

        init = (jnp.full((t, 1), -jnp.inf, F32), jnp.zeros((t, 1), F32), jnp.zeros((t, LANES), F32))
        carry = lax.fori_loop(0, i, functools.partial(step, masked=False), init)
        _, l, acc = step(i, carry, True)
        outs.append(acc / l)
    o_ref[...] = jnp.where(lane < HEAD_DIM, outs[0], outs[1]).astype(o_ref.dtype)


def _fox_attn(q, k, v, *, t):
    b, s, _ = q.shape
    return pl.pallas_call(
        functools.partial(_fox_attn_kernel, t=t),
        grid=(b, N_HEADS // 2, s // t),
        in_specs=[
            pl.BlockSpec((None, t, 2 * LANES), lambda bi, hp, i: (bi, i, hp)),
            pl.BlockSpec((None, s, 2 * LANES), lambda bi, hp, i: (bi, 0, hp)),
            pl.BlockSpec((None, s, LANES), lambda bi, hp, i: (bi, 0, hp)),
        ],
        out_specs=pl.BlockSpec((None, t, LANES), lambda bi, hp, i: (bi, i, hp)),
        out_shape=jax.ShapeDtypeStruct((b, s, BRANCH_WIDTH), BF16),
        compiler_params=_params("parallel", "parallel", "arbitrary"),
        name="fox_attn",
    )(q, k, v)


def _proj_dsa_kernel(h_ref, gmix_ref, w_ref, cos_ref, sin_ref, gq_ref, gk_ref,
                     dq_ref, iq_ref, kv_ref, ik_ref, iw_ref, *, tm):
    u = _rms(h_ref[...], gmix_ref[...]).astype(BF16)
    p = _dot(u, w_ref[...])
    qw = N_HEADS * LANES
    lane = lax.broadcasted_iota(I32, (tm, LANES), 1)
    cos = cos_ref[...]
    sin = sin_ref[...]

    def rope(x):
        rot = jnp.where(lane < HEAD_DIM // 2, pltpu.roll(x, LANES - HEAD_DIM // 2, 1), pltpu.roll(x, HEAD_DIM // 2, 1))
        return x * cos + rot * sin

    kv = p[:, 2 * qw:2 * qw + LANES]
    kk = jnp.where(lane < HEAD_DIM, kv, 0.0)
    kv_ref[...] = jnp.where(lane < HEAD_DIM, rope(_head_rms(kk, gk_ref[...])), kv).astype(BF16)

    ikw = p[:, 2 * qw + LANES:]
    ik_ref[...] = rope(jnp.where(lane < HEAD_DIM, ikw, 0.0)).astype(BF16)
    iw_ref[...] = ikw

    gq = gq_ref[...]
    iw_scale = N_HEADS ** -0.5 * HEAD_DIM ** -0.5
    for h in range(N_HEADS):
        dq = rope(_head_rms(p[:, h * LANES:(h + 1) * LANES], gq)) * (HEAD_DIM ** -0.5)
        dq_ref[:, h * LANES:(h + 1) * LANES] = dq.astype(BF16)
        w_h = ikw[:, HEAD_DIM + h:HEAD_DIM + h + 1] * iw_scale
        iq = rope(p[:, qw + h * LANES:qw + (h + 1) * LANES]) * w_h
        iq_ref[:, h * LANES:(h + 1) * LANES] = iq.astype(BF16)


def _proj_dsa(h, gmix, w, cos, sin, gq, gk, *, tm):
    b, s, d = h.shape
    qw = N_HEADS * LANES
    nw = w.shape[1]
    const = lambda bi, i: (0, 0)
    tok = lambda bi, i: (bi, i, 0)
    return pl.pallas_call(
        functools.partial(_proj_dsa_kernel, tm=tm),
        grid=(b, s // tm),
        in_specs=[
            pl.BlockSpec((None, tm, d), tok),
            pl.BlockSpec((1, d), const),
            pl.BlockSpec((d, nw), const),
            pl.BlockSpec((None, tm, LANES), tok),
            pl.BlockSpec((None, tm, LANES), tok),
            pl.BlockSpec((1, LANES), const),
            pl.BlockSpec((1, LANES), const),
        ],
        out_specs=[
            pl.BlockSpec((None, tm, qw), tok),
            pl.BlockSpec((None, tm, qw), tok),
            pl.BlockSpec((None, tm, LANES), tok),
            pl.BlockSpec((None, tm, LANES), tok),
            pl.BlockSpec((None, tm, LANES), tok),
        ],
        out_shape=[
            jax.ShapeDtypeStruct((b, s, qw), BF16),
            jax.ShapeDtypeStruct((b, s, qw), BF16),
            jax.ShapeDtypeStruct((b, s, LANES), BF16),
            jax.ShapeDtypeStruct((b, s, LANES), BF16),
            jax.ShapeDtypeStruct((b, s, LANES), F32),
        ],
        compiler_params=_params("parallel", "parallel"),
        name="proj_dsa",
    )(h, gmix, w, cos, sin, gq, gk)


def _dsa_attn_kernel(iq_ref, dq_ref, iw_ref, ik_ref, kv_ref, o_ref, key_ref, jsel_ref, *, tq, tk, k_sel, idx_bits):
    i = pl.program_id(1)
    nblk = ((i + 1) * tq + tk - 1) // tk
    groups = tk // LANES
    qpos = i * tq + lax.broadcasted_iota(I32, (tq, tk), 0)
    kcol = lax.broadcasted_iota(I32, (tq, tk), 1)

    def stack_heads(ref):
        return jnp.concatenate([ref[:, h * LANES:(h + 1) * LANES] for h in range(N_HEADS)], axis=0)

    iq = stack_heads(iq_ref)
    iw = iw_ref[...]
    pos_w = [iw[:, HEAD_DIM + h:HEAD_DIM + h + 1] > 0.0 for h in range(N_HEADS)]
    lo = jnp.concatenate([jnp.where(pw, 0.0, -jnp.inf) for pw in pos_w], axis=0)
    hi = jnp.concatenate([jnp.where(pw, jnp.inf, 0.0) for pw in pos_w], axis=0)

    def score_chunk(c, _):
        start = pl.multiple_of(c * tk, tk)
        x = _dot_nt(iq, ik_ref[pl.ds(start, tk), :])
        x = jnp.minimum(jnp.maximum(x, lo), hi)
        sc = x[0:tq]
        for h in range(1, N_HEADS):
            sc = sc + x[h * tq:(h + 1) * tq]
        sc = jnp.where(sc == 0.0, 0.0, sc)
        bits = lax.bitcast_convert_type(sc, I32)
        key = bits ^ ((bits >> 31) & 0x7FFFFFFF)
        key_ref[c] = jnp.where(c * tk + kcol <= qpos, key, INT_MIN)
        return 0

    lax.fori_loop(0, nblk, score_chunk, 0)

    def count(pred):
        def body(c, acc):
            hit = jnp.where(pred(key_ref[c], c), 1.0, 0.0)
            for g in range(groups):
                acc = acc + hit[:, g * LANES:(g + 1) * LANES]
            return acc
        acc = lax.fori_loop(0, nblk, body, jnp.zeros((tq, LANES), F32))
        return jnp.sum(acc, axis=-1, keepdims=True)

    def count_ge(cand):
        cand_b = jnp.broadcast_to(cand, (tq, tk))
        return count(lambda kk, c: kk >= cand_b)

    want = float(k_sel)
    thr = jnp.where(count_ge(jnp.zeros((tq, 1), I32)) >= want, 0, INT_MIN).astype(I32)

    def bit_step(b, thr):
        trial = thr + lax.shift_left(jnp.int32(1), 30 - b)
        return jnp.where(count_ge(trial) >= want, trial, thr)

    thr = lax.fori_loop(0, 31, bit_step, thr)
    thr = jnp.maximum(thr, INT_MIN + 1)

    n_gt = count_ge(thr + 1)
    need = want - n_gt
    n_eq = count_ge(thr) - n_gt
    jsel_ref[...] = jnp.full((tq, 1), 2 ** idx_bits, I32)
    thr_b = jnp.broadcast_to(thr, (tq, tk))

    @pl.when(jnp.max(n_eq - need) > 0.0)
    def _():
        def idx_step(b, pos):
            trial = pos + lax.shift_left(jnp.int32(1), idx_bits - 1 - b)
            trial_b = jnp.broadcast_to(trial, (tq, tk))
            below = count(lambda kk, c: (kk == thr_b) & (c * tk + kcol < trial_b))
            return jnp.where(below < need, trial, pos)
        jsel_ref[...] = lax.fori_loop(0, idx_bits, idx_step, jnp.zeros((tq, 1), I32))

    jsel_b = jnp.broadcast_to(jsel_ref[...], (tq, tk))

    dq = stack_heads(dq_ref)

    def attend(c, carry):
        m, l, acc = carry
        start = pl.multiple_of(c * tk, tk)
        kv = kv_ref[pl.ds(start, tk), :]
        kk = key_ref[c]
        sel = (kk > thr_b) | ((kk == thr_b) & (c * tk + kcol <= jsel_b))
        bias = jnp.where(sel, 0.0, NEG_BIG)
        s = _dot_nt(dq, kv) + jnp.concatenate([bias] * N_HEADS, axis=0)
        m_new = jnp.maximum(m, jnp.max(s, axis=-1, keepdims=True))
        p = jnp.exp(s - m_new)
        alpha = jnp.exp(m - m_new)
        l = alpha * l + jnp.sum(p, axis=-1, keepdims=True)
        acc = alpha * acc + _dot(p.astype(BF16), kv)
        return m_new, l, acc

    rows = N_HEADS * tq
    init = (jnp.full((rows, 1), NEG_BIG, F32), jnp.zeros((rows, 1), F32), jnp.zeros((rows, LANES), F32))
    _, l, acc = lax.fori_loop(0, nblk, attend, init)
    out = acc / l
    lane = lax.broadcasted_iota(I32, (tq, LANES), 1)
    for hp in range(N_HEADS // 2):
        even = pltpu.roll(out[(2 * hp) * tq:(2 * hp + 1) * tq], HEAD_DIM, 1)
        odd = out[(2 * hp + 1) * tq:(2 * hp + 2) * tq]
        o_ref[:, hp * LANES:(hp + 1) * LANES] = jnp.where(lane < HEAD_DIM, even, odd).astype(o_ref.dtype)


def _dsa_attn(iq, dq, iw, ik, kv, *, tq, tk, k_sel):
    b, s, qw = iq.shape
    idx_bits = max(1, int(np.ceil(np.log2(s))))
    tok = lambda bi, i: (bi, i, 0)
    seq = lambda bi, i: (bi, 0, 0)
    return pl.pallas_call(
        functools.partial(_dsa_attn_kernel, tq=tq, tk=tk, k_sel=k_sel, idx_bits=idx_bits),
        grid=(b, s // tq),
        in_specs=[
            pl.BlockSpec((None, tq, qw), tok),
            pl.BlockSpec((None, tq, qw), tok),
            pl.BlockSpec((None, tq, LANES), tok),
            pl.BlockSpec((None, s, LANES), seq),
            pl.BlockSpec((None, s, LANES), seq),
        ],
        out_specs=pl.BlockSpec((None, tq, BRANCH_WIDTH), tok),
        out_shape=jax.ShapeDtypeStruct((b, s, BRANCH_WIDTH), BF16),
        scratch_shapes=[pltpu.VMEM((s // tk, tq, tk), I32), pltpu.VMEM((tq, 1), I32)],
        compiler_params=_params("parallel", "arbitrary"),
        name="dsa_attn",
    )(iq, dq, iw, ik, kv)


def _mem_kv_kernel(mem_ref, g_ref, w_ref, gk_ref, km_ref, vm_ref):
    u = _rms(mem_ref[...], g_ref[...]).astype(BF16)
    p = _dot(u, w_ref[...])
    vm_ref[...] = p[:, BRANCH_WIDTH:].astype(BF16)
    for h in range(MEM_HEADS):
        sl = slice(h * MEM_HEAD_DIM, (h + 1) * MEM_HEAD_DIM)
        km_ref[:, sl] = _rms(p[:, sl], gk_ref[...]).astype(BF16)


def _mem_kv(mem, g, w, gk):
    b, m, d = mem.shape
    const = lambda bi: (0, 0)
    return pl.pallas_call(
        _mem_kv_kernel,
        grid=(b,),
        in_specs=[
            pl.BlockSpec((None, m, d), lambda bi: (bi, 0, 0)),
            pl.BlockSpec((1, d), const),
            pl.BlockSpec((d, 2 * BRANCH_WIDTH), const),
            pl.BlockSpec((1, MEM_HEAD_DIM), const),
        ],
        out_specs=[pl.BlockSpec((None, m, BRANCH_WIDTH), lambda bi: (bi, 0, 0))] * 2,
        out_shape=[jax.ShapeDtypeStruct((b, m, BRANCH_WIDTH), BF16)] * 2,
        compiler_params=_params("parallel"),
        name="mem_kv",
    )(mem, g, w, gk)


def _mem_attn_kernel(h_ref, gmix_ref, w_ref, gq_ref, km_ref, vm_ref, o_ref):
    u = _rms(h_ref[...], gmix_ref[...]).astype(BF16)
    q = _dot(u, w_ref[...])
    for h in range(MEM_HEADS):
        sl = slice(h * MEM_HEAD_DIM, (h + 1) * MEM_HEAD_DIM)
        qh = _rms(q[:, sl], gq_ref[...]).astype(BF16)
        s = _dot_nt(qh, km_ref[:, sl]) * (MEM_HEAD_DIM ** -0.5)
        p = jnp.exp(s - jnp.max(s, axis=-1, keepdims=True))
        p = p / jnp.sum(p, axis=-1, keepdims=True)
        o_ref[:, sl] = _dot(p.astype(BF16), vm_ref[:, sl]).astype(o_ref.dtype)


def _mem_attn(h, gmix, w, gq, km, vm, *, tm):
    b, s, d = h.shape
    m = km.shape[1]
    const = lambda bi, i: (0, 0)
    return pl.pallas_call(
        _mem_attn_kernel,
        grid=(b, s // tm),
        in_specs=[
            pl.BlockSpec((None, tm, d), lambda bi, i: (bi, i, 0)),
            pl.BlockSpec((1, d), const),
            pl.BlockSpec((d, BRANCH_WIDTH), const),
            pl.BlockSpec((1, MEM_HEAD_DIM), const),
            pl.BlockSpec((None, m, BRANCH_WIDTH), lambda bi, i: (bi, 0, 0)),
            pl.BlockSpec((None, m, BRANCH_WIDTH), lambda bi, i: (bi, 0, 0)),
        ],
        out_specs=pl.BlockSpec((None, tm, BRANCH_WIDTH), lambda bi, i: (bi, i, 0)),
        out_shape=jax.ShapeDtypeStruct((b, s, BRANCH_WIDTH), BF16),
        compiler_params=_params("parallel", "parallel"),
        name="mem_attn",
    )(h, gmix, w, gq, km, vm)


def _merge_kernel(h_ref, gmix_ref, oa_ref, ob_ref, oc_ref, wg_ref, wbr_ref, wout_ref, o_ref):
    x = h_ref[...]
    u = _rms(x, gmix_ref[...]).astype(BF16)
    d = x.shape[1]
    mixed = jnp.zeros(x.shape, F32)
    for n, br_ref in enumerate((oa_ref, ob_ref, oc_ref)):
        gate = jax.nn.sigmoid(_dot(u, wg_ref[:, n * d:(n + 1) * d]))
        mixed = mixed + gate * _dot(br_ref[...], wbr_ref[n])
    o_ref[...] = x + _dot(mixed.astype(BF16), wout_ref[...])


def _merge(h, gmix, oa, ob, oc, wg, wbr, wout, *, tm):
    n, d = h.shape
    tok = lambda i: (i, 0)
    const = lambda i: (0, 0)
    return pl.pallas_call(
        _merge_kernel,
        grid=(n // tm,),
        in_specs=[
            pl.BlockSpec((tm, d), tok),
            pl.BlockSpec((1, d), const),
            pl.BlockSpec((tm, BRANCH_WIDTH), tok),
            pl.BlockSpec((tm, BRANCH_WIDTH), tok),
            pl.BlockSpec((tm, BRANCH_WIDTH), tok),
            pl.BlockSpec((d, N_BRANCHES * d), const),
            pl.BlockSpec((N_BRANCHES, BRANCH_WIDTH, d), lambda i: (0, 0, 0)),
            pl.BlockSpec((d, d), const),
        ],
        out_specs=pl.BlockSpec((tm, d), tok),
        out_shape=jax.ShapeDtypeStruct((n, d), F32),
        compiler_params=_params("parallel"),
        name="merge",
    )(h, gmix, oa, ob, oc, wg, wbr, wout)


def _pad_heads(w, n_heads, head_dim):
    d = w.shape[0]
    w = w.reshape(d, n_heads, head_dim)
    return jnp.pad(w, ((0, 0), (0, 0), (0, LANES - head_dim))).reshape(d, n_heads * LANES)


def _pad_cols(w, width):
    return jnp.pad(w, ((0, 0), (0, width - w.shape[1])))


def _pad_gain(g):
    return jnp.pad(g, (0, LANES - g.shape[0])).reshape(1, LANES)


def _tile(n, pref):
    t = min(n, pref)
    assert n % t == 0, (n, t)
    return t


def kernel(x, mem, positions, ffn1_norm, ffn1_w_gate, ffn1_w_up, ffn1_w_down, mix_norm, mem_norm, w_in, b_forget,
           fox_q_norm, fox_k_norm, dsa_q_norm, dsa_k_norm, mem_q_norm, mem_k_norm, w_mem_kv, w_branch, w_out,
           ffn2_norm, ffn2_w_gate, ffn2_w_up, ffn2_w_down):
    b, s, d = x.shape
    depth = w_in.shape[0]
    n = b * s
    k_sel = min(TOPK_MAX, s // 4)
    bw = BRANCH_WIDTH

    inv_freq = ROPE_THETA ** (-jnp.arange(0, HEAD_DIM, 2, dtype=F32) / HEAD_DIM)
    ang = positions.astype(F32)[..., None] * inv_freq
    zeros = jnp.zeros((b, s, LANES - HEAD_DIM), F32)
    cos_t = jnp.concatenate([jnp.cos(ang), jnp.cos(ang), zeros], axis=-1)
    sin_t = jnp.concatenate([-jnp.sin(ang), jnp.sin(ang), zeros], axis=-1)

    tm_ffn = _tile(n, 1024)
    tm = _tile(s, 512)
    t_fox = _tile(s, 512)
    tq_dsa = _tile(s, 128)
    tk_dsa = _tile(s, 256)
    assert tk_dsa >= k_sel and tk_dsa % tq_dsa == 0

    h = x.reshape(n, d)
    for l in range(depth):
        h = _ffn(h, ffn1_norm[l].reshape(1, d), ffn1_w_gate[l].astype(BF16), ffn1_w_up[l].astype(BF16),
                 ffn1_w_down[l].astype(BF16), tm=tm_ffn, tf=256)

        w = w_in[l]
        sizes = (bw, bw, bw, N_HEADS, bw, HEAD_DIM, HEAD_DIM, N_HEADS * HEAD_DIM, HEAD_DIM, N_HEADS, bw, N_BRANCHES * d)
        offs = np.concatenate([[0], np.cumsum(sizes)])
        (w_fq, w_fk, w_fv, w_ff, w_dq, w_dk, w_dv, w_iq, w_ik, w_iw, w_mq, w_g) = [
            w[:, int(offs[j]):int(offs[j + 1])] for j in range(len(sizes))]
        w_fox = jnp.concatenate([_pad_heads(w_fq, N_HEADS, HEAD_DIM), _pad_heads(w_fk, N_HEADS, HEAD_DIM), w_fv,
                                 _pad_cols(w_ff, LANES)], axis=1).astype(BF16)
        w_dsa = jnp.concatenate([_pad_heads(w_dq, N_HEADS, HEAD_DIM), _pad_heads(w_iq, N_HEADS, HEAD_DIM), w_dk, w_dv,
                                 _pad_cols(jnp.concatenate([w_ik, w_iw], axis=1), LANES)], axis=1).astype(BF16)
        gmix = mix_norm[l].reshape(1, d)
        h3 = h.reshape(b, s, d)

        fq, fk, fv = _proj_fox(h3, gmix, w_fox, _pad_cols(b_forget[l].reshape(1, N_HEADS), LANES),
                               _pad_gain(fox_q_norm[l]), _pad_gain(fox_k_norm[l]), tm=tm)
        o_a = _fox_attn(fq, fk, fv, t=t_fox)

        dq, iq, kv, ik, iw = _proj_dsa(h3, gmix, w_dsa, cos_t, sin_t, _pad_gain(dsa_q_norm[l]),
                                       _pad_gain(dsa_k_norm[l]), tm=tm)
        o_b = _dsa_attn(iq, dq, iw, ik, kv, tq=tq_dsa, tk=tk_dsa, k_sel=k_sel)

        km, vm = _mem_kv(mem, mem_norm[l].reshape(1, d), w_mem_kv[l].astype(BF16), mem_k_norm[l].reshape(1, MEM_HEAD_DIM))
        o_c = _mem_attn(h3, gmix, w_mq.astype(BF16), mem_q_norm[l].reshape(1, MEM_HEAD_DIM), km, vm, tm=tm)

        h = _merge(h, gmix, o_a.reshape(n, bw), o_b.reshape(n, bw), o_c.reshape(n, bw), w_g.astype(BF16),
                   w_branch[l].astype(BF16), w_out[l].astype(BF16), tm=tm)

        h = _ffn(h, ffn2_norm[l].reshape(1, d), ffn2_w_gate[l].astype(BF16), ffn2_w_up[l].astype(BF16),
                 ffn2_w_down[l].astype(BF16), tm=tm_ffn, tf=256)
    return h.reshape(b, s, d)
```

```python
import functools

import jax
import jax.numpy as jnp
import numpy as np
from jax import lax
from jax.experimental import pallas as pl
from jax.experimental.pallas import tpu as pltpu

F32 = jnp.float32
BF16 = jnp.bfloat16
I32 = jnp.int32
I16 = jnp.int16

D_MODEL = 1024
N_HEADS = 8
HEAD_DIM = 64
MEM_HEADS = 4
MEM_HEAD_DIM = 128
BRANCH_WIDTH = 512
N_BRANCHES = 3
ROPE_THETA = 10000.0
TOPK_MAX = 256
EPS = 1e-6

LANES = 128
VMEM_LIMIT_BYTES = 56 * 1024 * 1024
INT_MIN = -(2 ** 31)
NEG_BIG = -1e30
LOG2E = 1.4426950408889634


def _params(*sem):
    return pltpu.CompilerParams(dimension_semantics=sem, vmem_limit_bytes=VMEM_LIMIT_BYTES)


def _rms(x, g):
    return x * lax.rsqrt(jnp.mean(x * x, axis=-1, keepdims=True) + EPS) * g


def _head_rms(x, g):
    ms = jnp.sum(x * x, axis=-1, keepdims=True) * (1.0 / HEAD_DIM)
    return x * lax.rsqrt(ms + EPS) * g


def _dot(a, b):
    return jnp.dot(a, b, preferred_element_type=F32)


def _dot_nt(a, b):
    return lax.dot_general(a, b, (((1,), (1,)), ((), ())), preferred_element_type=F32)


def _split3(x):
    a = x.astype(BF16)
    r = x - a.astype(F32)
    b = r.astype(BF16)
    c = (r - b.astype(F32)).astype(BF16)
    return a, b, c


def _ffn_kernel(x_ref, g_ref, wg_ref, wu_ref, wd_ref, o_ref, xn_ref, acc_ref):
    f = pl.program_id(1)

    @pl.when(f == 0)
    def _():
        xn_ref[...] = _rms(x_ref[...], g_ref[...]).astype(BF16)
        acc_ref[...] = jnp.zeros_like(acc_ref)

    xn = xn_ref[...]
    gate = _dot(xn, wg_ref[...])
    up = _dot(xn, wu_ref[...])
    act = gate * jax.nn.sigmoid(gate) * up
    acc_ref[...] += _dot(act.astype(BF16), wd_ref[...])

    @pl.when(f == pl.num_programs(1) - 1)
    def _():
        o_ref[...] = x_ref[...] + 0.5 * acc_ref[...]


def _ffn(x, g, wg, wu, wd, *, tm, tf):
    n, d = x.shape
    d_ff = wg.shape[1]
    return pl.pallas_call(
        _ffn_kernel,
        grid=(n // tm, d_ff // tf),
        in_specs=[
            pl.BlockSpec((tm, d), lambda i, f: (i, 0)),
            pl.BlockSpec((1, d), lambda i, f: (0, 0)),
            pl.BlockSpec((d, tf), lambda i, f: (0, f)),
            pl.BlockSpec((d, tf), lambda i, f: (0, f)),
            pl.BlockSpec((tf, d), lambda i, f: (f, 0)),
        ],
        out_specs=pl.BlockSpec((tm, d), lambda i, f: (i, 0)),
        out_shape=jax.ShapeDtypeStruct((n, d), F32),
        scratch_shapes=[pltpu.VMEM((tm, d), BF16), pltpu.VMEM((tm, d), F32)],
        compiler_params=_params("parallel", "arbitrary"),
        name="ffn",
    )(x, g, wg, wu, wd)


def _proj_fox_kernel(h_ref, gmix_ref, w_ref, bf_ref, gq_ref, gk_ref, q_ref, k_ref, v_ref, carry_ref, *, tm):
    @pl.when(pl.program_id(1) == 0)
    def _():
        carry_ref[...] = jnp.zeros_like(carry_ref)

    u = _rms(h_ref[...], gmix_ref[...]).astype(BF16)
    p = _dot(u, w_ref[...])
    qw = N_HEADS * LANES

    lane = lax.broadcasted_iota(I32, (tm, LANES), 1)
    z = p[:, 3 * qw:] + bf_ref[...]
    logf = -(jnp.maximum(-z, 0.0) + jnp.log1p(jnp.exp(-jnp.abs(z))))
    logf = jnp.where(lane < N_HEADS, logf, 0.0)

    row = lax.broadcasted_iota(I32, (tm, tm), 0)
    col = lax.broadcasted_iota(I32, (tm, tm), 1)
    tri = jnp.where(col <= row, 1.0, 0.0).astype(BF16)
    l1, l2, l3 = _split3(logf)
    c = _dot(tri, l1) + _dot(tri, l2) + _dot(tri, l3) + carry_ref[...]
    carry_ref[...] = c[tm - 1:tm, :]
    c = c * LOG2E

    gq = gq_ref[...]
    gk = gk_ref[...]
    for h in range(N_HEADS):
        qn = _head_rms(p[:, h * LANES:(h + 1) * LANES], gq) * (HEAD_DIM ** -0.5 * LOG2E)
        kn = _head_rms(p[:, qw + h * LANES:qw + (h + 1) * LANES], gk)
        c1, c2, c3 = (t.astype(F32) for t in _split3(c[:, h:h + 1]))
        qa = jnp.where(lane < 64, qn, jnp.where(lane == 64, c1, jnp.where(lane == 65, c2, jnp.where(
            lane == 66, c3, jnp.where(lane < 70, 1.0, 0.0)))))
        ka = jnp.where(lane < 64, kn, jnp.where(lane < 67, 1.0, jnp.where(lane == 67, -c1, jnp.where(
            lane == 68, -c2, jnp.where(lane == 69, -c3, 0.0)))))
        q_ref[:, h * LANES:(h + 1) * LANES] = qa.astype(BF16)
        k_ref[:, h * LANES:(h + 1) * LANES] = ka.astype(BF16)
        va = jnp.where(lane == HEAD_DIM, 1.0, p[:, 2 * qw + h * LANES:2 * qw + (h + 1) * LANES])
        v_ref[:, h * LANES:(h + 1) * LANES] = va.astype(BF16)


def _proj_fox(h, gmix, w, bf, gq, gk, *, tm):
    b, s, d = h.shape
    qw = N_HEADS * LANES
    nw = w.shape[1]
    const = lambda bi, i: (0, 0)
    return pl.pallas_call(
        functools.partial(_proj_fox_kernel, tm=tm),
        grid=(b, s // tm),
        in_specs=[
            pl.BlockSpec((None, tm, d), lambda bi, i: (bi, i, 0)),
            pl.BlockSpec((1, d), const),
            pl.BlockSpec((d, nw), const),
            pl.BlockSpec((1, LANES), const),
            pl.BlockSpec((1, LANES), const),
            pl.BlockSpec((1, LANES), const),
        ],
        out_specs=[
            pl.BlockSpec((None, tm, qw), lambda bi, i: (bi, i, 0)),
            pl.BlockSpec((None, tm, qw), lambda bi, i: (bi, i, 0)),
            pl.BlockSpec((None, tm, qw), lambda bi, i: (bi, i, 0)),
        ],
        out_shape=[jax.ShapeDtypeStruct((b, s, qw), BF16)] * 3,
        scratch_shapes=[pltpu.VMEM((1, LANES), F32)],
        compiler_params=_params("parallel", "arbitrary"),
        name="proj_fox",
    )(h, gmix, w, bf, gq, gk)


def _fox_attn_kernel(q_ref, k_ref, v_ref, o_ref, *, t):
    i = pl.program_id(2)
    lane = lax.broadcasted_iota(I32, (t, LANES), 1)
    row = lax.broadcasted_iota(I32, (t, t), 0)
    col = lax.broadcasted_iota(I32, (t, t), 1)

    def step(j, carry, masked):
        start = pl.multiple_of(j * t, t)
        new = []
        for hh in range(2):
            m, acc = carry[hh]
            sl = slice(hh * LANES, (hh + 1) * LANES)
            s = _dot_nt(q_ref[:, sl], k_ref[pl.ds(start, t), sl])
            if masked:
                s = jnp.where(col <= row, s, -jnp.inf)
            m_new = jnp.maximum(m, jnp.max(s, axis=-1, keepdims=True))
            p = jnp.exp2(s - m_new)
            acc = jnp.exp2(m - m_new) * acc + _dot(p.astype(BF16), v_ref[pl.ds(start, t), sl])
            new.append((m_new, acc))
        return tuple(new)

    init = tuple((jnp.full((t, 1), -jnp.inf, F32), jnp.zeros((t, LANES), F32)) for _ in range(2))
    carry = lax.fori_loop(0, i, functools.partial(step, masked=False), init)
    (_, acc0), (_, acc1) = step(i, carry, True)
    out0 = acc0 / acc0[:, HEAD_DIM:HEAD_DIM + 1]
    out1 = acc1 / acc1[:, HEAD_DIM:HEAD_DIM + 1]
    o_ref[...] = jnp.where(lane < HEAD_DIM, out0, pltpu.roll(out1, HEAD_DIM, 1)).astype(o_ref.dtype)


def _fox_attn(q, k, v, *, t):
    b, s, _ = q.shape
    return pl.pallas_call(
        functools.partial(_fox_attn_kernel, t=t),
        grid=(b, N_HEADS // 2, s // t),
        in_specs=[
            pl.BlockSpec((None, t, 2 * LANES), lambda bi, hp, i: (bi, i, hp)),
            pl.BlockSpec((None, s, 2 * LANES), lambda bi, hp, i: (bi, 0, hp)),
            pl.BlockSpec((None, s, 2 * LANES), lambda bi, hp, i: (bi, 0, hp)),
        ],
        out_specs=pl.BlockSpec((None, t, LANES), lambda bi, hp, i: (bi, i, hp)),
        out_shape=jax.ShapeDtypeStruct((b, s, BRANCH_WIDTH), BF16),
        compiler_params=_params("parallel", "parallel", "arbitrary"),
        name="fox_attn",
    )(q, k, v)


def _proj_dsa_kernel(h_ref, gmix_ref, w_ref, cos_ref, sin_ref, gq_ref, gk_ref,
                     dq_ref, iq_ref, k_ref, v_ref, ik_ref, iw_ref, *, tm):
    u = _rms(h_ref[...], gmix_ref[...]).astype(BF16)
    p = _dot(u, w_ref[...])
    qw = N_HEADS * LANES
    lane = lax.broadcasted_iota(I32, (tm, LANES), 1)
    cos = cos_ref[...]
    sin = sin_ref[...]

    def rope(x):
        rot = jnp.where(lane < HEAD_DIM // 2, pltpu.roll(x, LANES - HEAD_DIM // 2, 1), pltpu.roll(x, HEAD_DIM // 2, 1))
        return x * cos + rot * sin

    kv = p[:, 2 * qw:2 * qw + LANES]
    kk = jnp.where(lane < HEAD_DIM, kv, 0.0)
    k_ref[...] = rope(_head_rms(kk, gk_ref[...])).astype(BF16)
    v_ref[...] = jnp.where(lane < HEAD_DIM, pltpu.roll(kv, HEAD_DIM, 1),
                           jnp.where(lane == HEAD_DIM, 1.0, 0.0)).astype(BF16)

    ikw = p[:, 2 * qw + LANES:]
    ik_ref[...] = rope(jnp.where(lane < HEAD_DIM, ikw, 0.0)).astype(BF16)
    iw_ref[...] = ikw

    gq = gq_ref[...]
    iw_scale = N_HEADS ** -0.5 * HEAD_DIM ** -0.5
    for h in range(N_HEADS):
        dq = rope(_head_rms(p[:, h * LANES:(h + 1) * LANES], gq)) * (HEAD_DIM ** -0.5 * LOG2E)
        dq_ref[:, h * LANES:(h + 1) * LANES] = dq.astype(BF16)
        w_h = ikw[:, HEAD_DIM + h:HEAD_DIM + h + 1] * iw_scale
        iq = rope(p[:, qw + h * LANES:qw + (h + 1) * LANES]) * w_h
        iq_ref[:, h * LANES:(h + 1) * LANES] = iq.astype(BF16)


def _proj_dsa(h, gmix, w, cos, sin, gq, gk, *, tm):
    b, s, d = h.shape
    qw = N_HEADS * LANES
    nw = w.shape[1]
    const = lambda bi, i: (0, 0)
    tok = lambda bi, i: (bi, i, 0)
    return pl.pallas_call(
        functools.partial(_proj_dsa_kernel, tm=tm),
        grid=(b, s // tm),
        in_specs=[
            pl.BlockSpec((None, tm, d), tok),
            pl.BlockSpec((1, d), const),
            pl.BlockSpec((d, nw), const),
            pl.BlockSpec((None, tm, LANES), tok),
            pl.BlockSpec((None, tm, LANES), tok),
            pl.BlockSpec((1, LANES), const),
            pl.BlockSpec((1, LANES), const),
        ],
        out_specs=[
            pl.BlockSpec((None, tm, qw), tok),
            pl.BlockSpec((None, tm, qw), tok),
            pl.BlockSpec((None, tm, LANES), tok),
            pl.BlockSpec((None, tm, LANES), tok),
            pl.BlockSpec((None, tm, LANES), tok),
            pl.BlockSpec((None, tm, LANES), tok),
        ],
        out_shape=[
            jax.ShapeDtypeStruct((b, s, qw), BF16),
            jax.ShapeDtypeStruct((b, s, qw), BF16),
            jax.ShapeDtypeStruct((b, s, LANES), BF16),
            jax.ShapeDtypeStruct((b, s, LANES), BF16),
            jax.ShapeDtypeStruct((b, s, LANES), BF16),
            jax.ShapeDtypeStruct((b, s, LANES), F32),
        ],
        compiler_params=_params("parallel", "parallel"),
        name="proj_dsa",
    )(h, gmix, w, cos, sin, gq, gk)


def _dsa_attn_kernel(iq_ref, dq_ref, iw_ref, ik_ref, k_ref, v_ref, o_ref,
                     key_ref, hi_ref, lo_ref, bound_ref, jsel_ref, m_ref, acc_ref, *, tq, tk, k_sel, idx_bits):
    i = pl.program_id(1)
    npair = ((i + 1) * tq + 2 * tk - 1) // (2 * tk)
    nblk = 2 * npair
    groups = tk // LANES
    qpos = i * tq + lax.broadcasted_iota(I32, (tq, tk), 0)
    kcol = lax.broadcasted_iota(I32, (tq, tk), 1)

    def stack_heads(ref):
        return jnp.concatenate([ref[:, h * LANES:(h + 1) * LANES] for h in range(N_HEADS)], axis=0)

    iw = iw_ref[...]
    for h in range(N_HEADS):
        pos_w = iw[:, HEAD_DIM + h:HEAD_DIM + h + 1] > 0.0
        bound_ref[0, h] = jnp.broadcast_to(jnp.where(pos_w, 0.0, -jnp.inf), (tq, LANES))
        bound_ref[1, h] = jnp.broadcast_to(jnp.where(pos_w, jnp.inf, 0.0), (tq, LANES))
    iq = stack_heads(iq_ref)

    def score_pair(cp, _):
        for u in range(2):
            c = 2 * cp + u
            start = pl.multiple_of(c * tk, tk)
            x = _dot_nt(iq, ik_ref[pl.ds(start, tk), :])
            sc = None
            for h in range(N_HEADS):
                lo = jnp.concatenate([bound_ref[0, h]] * groups, axis=1)
                hi = jnp.concatenate([bound_ref[1, h]] * groups, axis=1)
                t = jnp.minimum(jnp.maximum(x[h * tq:(h + 1) * tq], lo), hi)
                sc = t if sc is None else sc + t
            sc = jnp.where(sc == 0.0, 0.0, sc)
            bits = lax.bitcast_convert_type(sc, I32)
            key = bits ^ ((bits >> 31) & 0x7FFFFFFF)
            key = jnp.where(c * tk + kcol <= qpos, key, INT_MIN)
            key_ref[c] = key
            hi_ref[c] = (key >> 16).astype(I16)
            lo_ref[c] = ((key & 0xFFFF) - 32768).astype(I16)
        return 0

    lax.fori_loop(0, npair, score_pair, 0)

    def count(pred):
        def body(c, acc):
            hit = jnp.where(pred(key_ref[c], c), 1.0, 0.0)
            for g in range(groups):
                acc = acc + hit[:, g * LANES:(g + 1) * LANES]
            return acc
        acc = lax.fori_loop(0, nblk, body, jnp.zeros((tq, LANES), F32))
        return jnp.sum(acc, axis=-1, keepdims=True)

    one16 = jnp.ones((tq, tk), BF16)
    zero16 = jnp.zeros((tq, tk), BF16)

    def count16(ref, preds):
        def body(cp, accs):
            for u in range(2):
                v = ref[2 * cp + u]
                new = []
                for a, p in zip(accs, preds):
                    hit = jnp.where(p(v), one16, zero16)
                    for g in range(groups):
                        a = a + hit[:, g * LANES:(g + 1) * LANES]
                    new.append(a)
                accs = tuple(new)
            return accs
        accs = lax.fori_loop(0, npair, body, tuple(jnp.zeros((tq, LANES), BF16) for _ in preds))
        return [jnp.sum(a.astype(F32), axis=-1, keepdims=True) for a in accs]

    def bcast16(x):
        return jnp.broadcast_to(x.astype(I16), (tq, tk))

    def radix16(ref, want):
        def two_bits(j, prefix):
            step = lax.shift_left(jnp.int32(1), 14 - 2 * j)
            cands = [bcast16(prefix + k * step) for k in (1, 2, 3)]
            counts = count16(ref, [lambda v, cb=cb: v >= cb for cb in cands])
            inc = sum(jnp.where(n >= want, 1, 0) for n in counts)
            return prefix + inc * step
        return lax.fori_loop(0, 8, two_bits, jnp.full((tq, 1), -32768, I32))

    want = float(k_sel)
    thi = radix16(hi_ref, want)
    thi_b = bcast16(thi)
    n_ge_hi, n_gt_hi = count16(hi_ref, [lambda v: v >= thi_b, lambda v: v > thi_b])

    def keep_bracket(c, _):
        lo_ref[c] = jnp.where(hi_ref[c] == thi_b, lo_ref[c], -32768)
        return 0

    lax.fori_loop(0, nblk, keep_bracket, 0)
    tlo = radix16(lo_ref, want - n_gt_hi)
    tlo_b = bcast16(tlo)
    n_gt_lo, n_eq = count16(lo_ref, [lambda v: v > tlo_b, lambda v: v == tlo_b])
    n_fill = (nblk * tk).astype(F32) - (n_ge_hi - n_gt_hi)
    n_eq = n_eq - jnp.where(tlo == -32768, n_fill, 0.0)
    n_gt = n_gt_hi + n_gt_lo
    thr = thi * 65536 + (tlo + 32768)
    few = thr == INT_MIN
    thr = jnp.maximum(thr, INT_MIN + 1)

    need = want - n_gt
    jsel_ref[...] = jnp.full((tq, 1), 2 ** idx_bits, I32)
    thr_b = jnp.broadcast_to(thr, (tq, tk))

    @pl.when(jnp.max(jnp.where(few, 0.0, n_eq - need)) > 0.0)
    def _():
        def idx_step(b, pos):
            trial = pos + lax.shift_left(jnp.int32(1), idx_bits - 1 - b)
            trial_b = jnp.broadcast_to(trial, (tq, tk))
            below = count(lambda kk, c: (kk == thr_b) & (c * tk + kcol < trial_b))
            return jnp.where(below < need, trial, pos)
        jsel_ref[...] = lax.fori_loop(0, idx_bits, idx_step, jnp.zeros((tq, 1), I32))

    jsel_b = jnp.broadcast_to(jsel_ref[...], (tq, tk))

    dq = stack_heads(dq_ref)
    m_ref[...] = jnp.full(m_ref.shape, NEG_BIG, F32)
    acc_ref[...] = jnp.zeros(acc_ref.shape, F32)

    def attend_pair(cp, _):
        m_old = [m_ref[h] for h in range(N_HEADS)]
        m_run = list(m_old)
        pvs = []
        for u in range(2):
            c = 2 * cp + u
            start = pl.multiple_of(c * tk, tk)
            kk = key_ref[c]
            sel = (kk > thr_b) | ((kk == thr_b) & (c * tk + kcol <= jsel_b))
            bias = jnp.where(sel, 0.0, NEG_BIG)
            s = _dot_nt(dq, k_ref[pl.ds(start, tk), :])
            ps = []
            for h in range(N_HEADS):
                sh = s[h * tq:(h + 1) * tq] + bias
                m_new = jnp.maximum(m_run[h], jnp.broadcast_to(jnp.max(sh, axis=-1, keepdims=True), (tq, LANES)))
                ps.append(jnp.exp2(sh - jnp.concatenate([m_new] * groups, axis=1)).astype(BF16))
                if u == 1:
                    pvs[h] = pvs[h] * jnp.exp2(m_run[h] - m_new)
                m_run[h] = m_new
            pv = _dot(jnp.concatenate(ps, axis=0), v_ref[pl.ds(start, tk), :])
            if u == 0:
                pvs = [pv[h * tq:(h + 1) * tq] for h in range(N_HEADS)]
            else:
                pvs = [pvs[h] + pv[h * tq:(h + 1) * tq] for h in range(N_HEADS)]
        for h in range(N_HEADS):
            m_ref[h] = m_run[h]
            acc_ref[h] = jnp.exp2(m_old[h] - m_run[h]) * acc_ref[h] + pvs[h]
        return 0

    lax.fori_loop(0, npair, attend_pair, 0)
    lane = lax.broadcasted_iota(I32, (tq, LANES), 1)
    outs = []
    for h in range(N_HEADS):
        acc = acc_ref[h]
        outs.append(acc / acc[:, HEAD_DIM:HEAD_DIM + 1])
    for hp in range(N_HEADS // 2):
        odd = pltpu.roll(outs[2 * hp + 1], HEAD_DIM, 1)
        o_ref[:, hp * LANES:(hp + 1) * LANES] = jnp.where(lane < HEAD_DIM, outs[2 * hp], odd).astype(o_ref.dtype)


def _dsa_attn(iq, dq, iw, ik, k, v, *, tq, tk, k_sel):
    b, s, qw = iq.shape
    idx_bits = max(1, int(np.ceil(np.log2(s))))
    nchunk = s // tk
    assert nchunk % 2 == 0 and nchunk * (tk // LANES) <= 256
    tok = lambda bi, i: (bi, i, 0)
    seq = lambda bi, i: (bi, 0, 0)
    return pl.pallas_call(
        functools.partial(_dsa_attn_kernel, tq=tq, tk=tk, k_sel=k_sel, idx_bits=idx_bits),
        grid=(b, s // tq),
        in_specs=[
            pl.BlockSpec((None, tq, qw), tok),
            pl.BlockSpec((None, tq, qw), tok),
            pl.BlockSpec((None, tq, LANES), tok),
            pl.BlockSpec((None, s, LANES), seq),
            pl.BlockSpec((None, s, LANES), seq),
            pl.BlockSpec((None, s, LANES), seq),
        ],
        out_specs=pl.BlockSpec((None, tq, BRANCH_WIDTH), tok),
        out_shape=jax.ShapeDtypeStruct((b, s, BRANCH_WIDTH), BF16),
        scratch_shapes=[
            pltpu.VMEM((nchunk, tq, tk), I32),
            pltpu.VMEM((nchunk, tq, tk), I16),
            pltpu.VMEM((nchunk, tq, tk), I16),
            pltpu.VMEM((2, N_HEADS, tq, LANES), F32),
            pltpu.VMEM((tq, 1), I32),
            pltpu.VMEM((N_HEADS, tq, LANES), F32),
            pltpu.VMEM((N_HEADS, tq, LANES), F32),
        ],
        compiler_params=_params("parallel", "arbitrary"),
        name="dsa_attn",
    )(iq, dq, iw, ik, k, v)


def _mem_kv_kernel(mem_ref, g_ref, w_ref, gk_ref, km_ref, vm_ref):
    u = _rms(mem_ref[...], g_ref[...]).astype(BF16)
    p = _dot(u, w_ref[...])
    vm_ref[...] = p[:, BRANCH_WIDTH:].astype(BF16)
    for h in range(MEM_HEADS):
        sl = slice(h * MEM_HEAD_DIM, (h + 1) * MEM_HEAD_DIM)
        km_ref[:, sl] = _rms(p[:, sl], gk_ref[...]).astype(BF16)


def _mem_kv(mem, g, w, gk):
    b, m, d = mem.shape
    const = lambda bi: (0, 0)
    return pl.pallas_call(
        _mem_kv_kernel,
        grid=(b,),
        in_specs=[
            pl.BlockSpec((None, m, d), lambda bi: (bi, 0, 0)),
            pl.BlockSpec((1, d), const),
            pl.BlockSpec((d, 2 * BRANCH_WIDTH), const),
            pl.BlockSpec((1, MEM_HEAD_DIM), const),
        ],
        out_specs=[pl.BlockSpec((None, m, BRANCH_WIDTH), lambda bi: (bi, 0, 0))] * 2,
        out_shape=[jax.ShapeDtypeStruct((b, m, BRANCH_WIDTH), BF16)] * 2,
        compiler_params=_params("parallel"),
        name="mem_kv",
    )(mem, g, w, gk)


def _mem_attn_kernel(h_ref, gmix_ref, w_ref, gq_ref, km_ref, vm_ref, o_ref):
    u = _rms(h_ref[...], gmix_ref[...]).astype(BF16)
    q = _dot(u, w_ref[...])
    for h in range(MEM_HEADS):
        sl = slice(h * MEM_HEAD_DIM, (h + 1) * MEM_HEAD_DIM)
        qh = _rms(q[:, sl], gq_ref[...]).astype(BF16)
        s = _dot_nt(qh, km_ref[:, sl]) * (MEM_HEAD_DIM ** -0.5)
        p = jnp.exp(s - jnp.max(s, axis=-1, keepdims=True))
        p = p / jnp.sum(p, axis=-1, keepdims=True)
        o_ref[:, sl] = _dot(p.astype(BF16), vm_ref[:, sl]).astype(o_ref.dtype)


def _mem_attn(h, gmix, w, gq, km, vm, *, tm):
    b, s, d = h.shape
    m = km.shape[1]
    const = lambda bi, i: (0, 0)
    return pl.pallas_call(
        _mem_attn_kernel,
        grid=(b, s // tm),
        in_specs=[
            pl.BlockSpec((None, tm, d), lambda bi, i: (bi, i, 0)),
            pl.BlockSpec((1, d), const),
            pl.BlockSpec((d, BRANCH_WIDTH), const),
            pl.BlockSpec((1, MEM_HEAD_DIM), const),
            pl.BlockSpec((None, m, BRANCH_WIDTH), lambda bi, i: (bi, 0, 0)),
            pl.BlockSpec((None, m, BRANCH_WIDTH), lambda bi, i: (bi, 0, 0)),
        ],
        out_specs=pl.BlockSpec((None, tm, BRANCH_WIDTH), lambda bi, i: (bi, i, 0)),
        out_shape=jax.ShapeDtypeStruct((b, s, BRANCH_WIDTH), BF16),
        compiler_params=_params("parallel", "parallel"),
        name="mem_attn",
    )(h, gmix, w, gq, km, vm)


def _merge_kernel(h_ref, gmix_ref, oa_ref, ob_ref, oc_ref, wg_ref, wbr_ref, wout_ref, o_ref):
    x = h_ref[...]
    u = _rms(x, gmix_ref[...]).astype(BF16)
    d = x.shape[1]
    mixed = jnp.zeros(x.shape, F32)
    for n, br_ref in enumerate((oa_ref, ob_ref, oc_ref)):
        gate = jax.nn.sigmoid(_dot(u, wg_ref[:, n * d:(n + 1) * d]))
        mixed = mixed + gate * _dot(br_ref[...], wbr_ref[n])
    o_ref[...] = x + _dot(mixed.astype(BF16), wout_ref[...])


def _merge(h, gmix, oa, ob, oc, wg, wbr, wout, *, tm):
    n, d = h.shape
    tok = lambda i: (i, 0)
    const = lambda i: (0, 0)
    return pl.pallas_call(
        _merge_kernel,
        grid=(n // tm,),
        in_specs=[
            pl.BlockSpec((tm, d), tok),
            pl.BlockSpec((1, d), const),
            pl.BlockSpec((tm, BRANCH_WIDTH), tok),
            pl.BlockSpec((tm, BRANCH_WIDTH), tok),
            pl.BlockSpec((tm, BRANCH_WIDTH), tok),
            pl.BlockSpec((d, N_BRANCHES * d), const),
            pl.BlockSpec((N_BRANCHES, BRANCH_WIDTH, d), lambda i: (0, 0, 0)),
            pl.BlockSpec((d, d), const),
        ],
        out_specs=pl.BlockSpec((tm, d), tok),
        out_shape=jax.ShapeDtypeStruct((n, d), F32),
        compiler_params=_params("parallel"),
        name="merge",
    )(h, gmix, oa, ob, oc, wg, wbr, wout)


def _pad_heads(w, n_heads, head_dim):
    d = w.shape[0]
    w = w.reshape(d, n_heads, head_dim)
    return jnp.pad(w, ((0, 0), (0, 0), (0, LANES - head_dim))).reshape(d, n_heads * LANES)


def _pad_cols(w, width):
    return jnp.pad(w, ((0, 0), (0, width - w.shape[1])))


def _pad_gain(g):
    return jnp.pad(g, (0, LANES - g.shape[0])).reshape(1, LANES)


def _tile(n, pref):
    t = min(n, pref)
    assert n % t == 0, (n, t)
    return t


def kernel(x, mem, positions, ffn1_norm, ffn1_w_gate, ffn1_w_up, ffn1_w_down, mix_norm, mem_norm, w_in, b_forget,
           fox_q_norm, fox_k_norm, dsa_q_norm, dsa_k_norm, mem_q_norm, mem_k_norm, w_mem_kv, w_branch, w_out,
           ffn2_norm, ffn2_w_gate, ffn2_w_up, ffn2_w_down):
    b, s, d = x.shape
    depth = w_in.shape[0]
    n = b * s
    k_sel = min(TOPK_MAX, s // 4)
    bw = BRANCH_WIDTH

    inv_freq = ROPE_THETA ** (-jnp.arange(0, HEAD_DIM, 2, dtype=F32) / HEAD_DIM)
    ang = positions.astype(F32)[..., None] * inv_freq
    zeros = jnp.zeros((b, s, LANES - HEAD_DIM), F32)
    cos_t = jnp.concatenate([jnp.cos(ang), jnp.cos(ang), zeros], axis=-1)
    sin_t = jnp.concatenate([-jnp.sin(ang), jnp.sin(ang), zeros], axis=-1)

    tm_ffn = _tile(n, 1024)
    tm = _tile(s, 512)
    t_fox = _tile(s, 512)
    tq_dsa = _tile(s, 128)
    tk_dsa = _tile(s, 256)
    assert tk_dsa >= k_sel and tk_dsa % tq_dsa == 0

    h = x.reshape(n, d)
    for l in range(depth):
        h = _ffn(h, ffn1_norm[l].reshape(1, d), ffn1_w_gate[l].astype(BF16), ffn1_w_up[l].astype(BF16),
                 ffn1_w_down[l].astype(BF16), tm=tm_ffn, tf=256)

        w = w_in[l]
        sizes = (bw, bw, bw, N_HEADS, bw, HEAD_DIM, HEAD_DIM, N_HEADS * HEAD_DIM, HEAD_DIM, N_HEADS, bw, N_BRANCHES * d)
        offs = np.concatenate([[0], np.cumsum(sizes)])
        (w_fq, w_fk, w_fv, w_ff, w_dq, w_dk, w_dv, w_iq, w_ik, w_iw, w_mq, w_g) = [
            w[:, int(offs[j]):int(offs[j + 1])] for j in range(len(sizes))]
        w_fox = jnp.concatenate([_pad_heads(w_fq, N_HEADS, HEAD_DIM), _pad_heads(w_fk, N_HEADS, HEAD_DIM),
                                 _pad_heads(w_fv, N_HEADS, HEAD_DIM), _pad_cols(w_ff, LANES)], axis=1).astype(BF16)
        w_dsa = jnp.concatenate([_pad_heads(w_dq, N_HEADS, HEAD_DIM), _pad_heads(w_iq, N_HEADS, HEAD_DIM), w_dk, w_dv,
                                 _pad_cols(jnp.concatenate([w_ik, w_iw], axis=1), LANES)], axis=1).astype(BF16)
        gmix = mix_norm[l].reshape(1, d)
        h3 = h.reshape(b, s, d)

        fq, fk, fv = _proj_fox(h3, gmix, w_fox, _pad_cols(b_forget[l].reshape(1, N_HEADS), LANES),
                               _pad_gain(fox_q_norm[l]), _pad_gain(fox_k_norm[l]), tm=tm)
        o_a = _fox_attn(fq, fk, fv, t=t_fox)

        dq, iq, dk, dv, ik, iw = _proj_dsa(h3, gmix, w_dsa, cos_t, sin_t, _pad_gain(dsa_q_norm[l]),
                                           _pad_gain(dsa_k_norm[l]), tm=tm)
        o_b = _dsa_attn(iq, dq, iw, ik, dk, dv, tq=tq_dsa, tk=tk_dsa, k_sel=k_sel)

        km, vm = _mem_kv(mem, mem_norm[l].reshape(1, d), w_mem_kv[l].astype(BF16), mem_k_norm[l].reshape(1, MEM_HEAD_DIM))
        o_c = _mem_attn(h3, gmix, w_mq.astype(BF16), mem_q_norm[l].reshape(1, MEM_HEAD_DIM), km, vm, tm=tm)

        h = _merge(h, gmix, o_a.reshape(n, bw), o_b.reshape(n, bw), o_c.reshape(n, bw), w_g.astype(BF16),
                   w_branch[l].astype(BF16), w_out[l].astype(BF16), tm=tm)

        h = _ffn(h, ffn2_norm[l].reshape(1, d), ffn2_w_gate[l].astype(BF16), ffn2_w_up[l].astype(BF16),
                 ffn2_w_down[l].astype(BF16), tm=tm_ffn, tf=256)
    return h.reshape(b, s, d)
```

```python
import functools

import jax
import jax.numpy as jnp
import numpy as np
from jax import lax
from jax.experimental import pallas as pl
from jax.experimental.pallas import tpu as pltpu

F32 = jnp.float32
BF16 = jnp.bfloat16
I32 = jnp.int32

D_MODEL = 1024
N_HEADS = 8
HEAD_DIM = 64
MEM_HEADS = 4
MEM_HEAD_DIM = 128
BRANCH_WIDTH = 512
N_BRANCHES = 3
ROPE_THETA = 10000.0
TOPK_MAX = 256
EPS = 1e-6

LANES = 128
VMEM_LIMIT_BYTES = 56 * 1024 * 1024
INT_MIN = -(2 ** 31)
NEG_BIG = -1e30
LOG2E = 1.4426950408889634


def _params(*sem):
    return pltpu.CompilerParams(dimension_semantics=sem, vmem_limit_bytes=VMEM_LIMIT_BYTES)


def _rms(x, g):
    return x * lax.rsqrt(jnp.mean(x * x, axis=-1, keepdims=True) + EPS) * g


def _head_rms(x, g):
    ms = jnp.sum(x * x, axis=-1, keepdims=True) * (1.0 / HEAD_DIM)
    return x * lax.rsqrt(ms + EPS) * g


def _dot(a, b):
    return jnp.dot(a, b, preferred_element_type=F32)


def _dot_nt(a, b):
    return lax.dot_general(a, b, (((1,), (1,)), ((), ())), preferred_element_type=F32)


def _split3(x):
    a = x.astype(BF16)
    r = x - a.astype(F32)
    b = r.astype(BF16)
    c = (r - b.astype(F32)).astype(BF16)
    return a, b, c


def _ffn_kernel(x_ref, g_ref, wg_ref, wu_ref, wd_ref, o_ref, xn_ref, acc_ref):
    f = pl.program_id(1)

    @pl.when(f == 0)
    def _():
        xn_ref[...] = _rms(x_ref[...], g_ref[...]).astype(BF16)
        acc_ref[...] = jnp.zeros_like(acc_ref)

    xn = xn_ref[...]
    gate = _dot(xn, wg_ref[...])
    up = _dot(xn, wu_ref[...])
    act = gate * jax.nn.sigmoid(gate) * up
    acc_ref[...] += _dot(act.astype(BF16), wd_ref[...])

    @pl.when(f == pl.num_programs(1) - 1)
    def _():
        o_ref[...] = x_ref[...] + 0.5 * acc_ref[...]


def _ffn(x, g, wg, wu, wd, *, tm, tf):
    n, d = x.shape
    d_ff = wg.shape[1]
    return pl.pallas_call(
        _ffn_kernel,
        grid=(n // tm, d_ff // tf),
        in_specs=[
            pl.BlockSpec((tm, d), lambda i, f: (i, 0)),
            pl.BlockSpec((1, d), lambda i, f: (0, 0)),
            pl.BlockSpec((d, tf), lambda i, f: (0, f)),
            pl.BlockSpec((d, tf), lambda i, f: (0, f)),
            pl.BlockSpec((tf, d), lambda i, f: (f, 0)),
        ],
        out_specs=pl.BlockSpec((tm, d), lambda i, f: (i, 0)),
        out_shape=jax.ShapeDtypeStruct((n, d), F32),
        scratch_shapes=[pltpu.VMEM((tm, d), BF16), pltpu.VMEM((tm, d), F32)],
        compiler_params=_params("parallel", "arbitrary"),
        name="ffn",
    )(x, g, wg, wu, wd)


def _proj_fox_kernel(h_ref, gmix_ref, w_ref, bf_ref, gq_ref, gk_ref, q_ref, k_ref, v_ref, carry_ref, *, tm):
    @pl.when(pl.program_id(1) == 0)
    def _():
        carry_ref[...] = jnp.zeros_like(carry_ref)

    u = _rms(h_ref[...], gmix_ref[...]).astype(BF16)
    p = _dot(u, w_ref[...])
    qw = N_HEADS * LANES

    lane = lax.broadcasted_iota(I32, (tm, LANES), 1)
    z = p[:, 3 * qw:] + bf_ref[...]
    logf = -(jnp.maximum(-z, 0.0) + jnp.log1p(jnp.exp(-jnp.abs(z))))
    logf = jnp.where(lane < N_HEADS, logf, 0.0)

    row = lax.broadcasted_iota(I32, (tm, tm), 0)
    col = lax.broadcasted_iota(I32, (tm, tm), 1)
    tri = jnp.where(col <= row, 1.0, 0.0).astype(BF16)
    l1, l2, l3 = _split3(logf)
    c = _dot(tri, l1) + _dot(tri, l2) + _dot(tri, l3) + carry_ref[...]
    carry_ref[...] = c[tm - 1:tm, :]
    c = c * LOG2E

    gq = gq_ref[...]
    gk = gk_ref[...]
    for h in range(N_HEADS):
        qn = _head_rms(p[:, h * LANES:(h + 1) * LANES], gq) * (HEAD_DIM ** -0.5 * LOG2E)
        kn = _head_rms(p[:, qw + h * LANES:qw + (h + 1) * LANES], gk)
        c1, c2, c3 = (t.astype(F32) for t in _split3(c[:, h:h + 1]))
        qa = jnp.where(lane < 64, qn, jnp.where(lane == 64, c1, jnp.where(lane == 65, c2, jnp.where(
            lane == 66, c3, jnp.where(lane < 70, 1.0, 0.0)))))
        ka = jnp.where(lane < 64, kn, jnp.where(lane < 67, 1.0, jnp.where(lane == 67, -c1, jnp.where(
            lane == 68, -c2, jnp.where(lane == 69, -c3, 0.0)))))
        q_ref[:, h * LANES:(h + 1) * LANES] = qa.astype(BF16)
        k_ref[:, h * LANES:(h + 1) * LANES] = ka.astype(BF16)
        va = jnp.where(lane == HEAD_DIM, 1.0, p[:, 2 * qw + h * LANES:2 * qw + (h + 1) * LANES])
        v_ref[:, h * LANES:(h + 1) * LANES] = va.astype(BF16)


def _proj_fox(h, gmix, w, bf, gq, gk, *, tm):
    b, s, d = h.shape
    qw = N_HEADS * LANES
    nw = w.shape[1]
    const = lambda bi, i: (0, 0)
    return pl.pallas_call(
        functools.partial(_proj_fox_kernel, tm=tm),
        grid=(b, s // tm),
        in_specs=[
            pl.BlockSpec((None, tm, d), lambda bi, i: (bi, i, 0)),
            pl.BlockSpec((1, d), const),
            pl.BlockSpec((d, nw), const),
            pl.BlockSpec((1, LANES), const),
            pl.BlockSpec((1, LANES), const),
            pl.BlockSpec((1, LANES), const),
        ],
        out_specs=[
            pl.BlockSpec((None, tm, qw), lambda bi, i: (bi, i, 0)),
            pl.BlockSpec((None, tm, qw), lambda bi, i: (bi, i, 0)),
            pl.BlockSpec((None, tm, qw), lambda bi, i: (bi, i, 0)),
        ],
        out_shape=[jax.ShapeDtypeStruct((b, s, qw), BF16)] * 3,
        scratch_shapes=[pltpu.VMEM((1, LANES), F32)],
        compiler_params=_params("parallel", "arbitrary"),
        name="proj_fox",
    )(h, gmix, w, bf, gq, gk)


def _fox_attn_kernel(q_ref, k_ref, v_ref, o_ref, *, t):
    i = pl.program_id(2)
    lane = lax.broadcasted_iota(I32, (t, LANES), 1)
    row = lax.broadcasted_iota(I32, (t, t), 0)
    col = lax.broadcasted_iota(I32, (t, t), 1)

    def step(j, carry, masked):
        start = pl.multiple_of(j * t, t)
        new = []
        for hh in range(2):
            m, acc = carry[hh]
            sl = slice(hh * LANES, (hh + 1) * LANES)
            s = _dot_nt(q_ref[:, sl], k_ref[pl.ds(start, t), sl])
            if masked:
                s = jnp.where(col <= row, s, -jnp.inf)
            m_new = jnp.maximum(m, jnp.max(s, axis=-1, keepdims=True))
            p = jnp.exp2(s - m_new)
            acc = jnp.exp2(m - m_new) * acc + _dot(p.astype(BF16), v_ref[pl.ds(start, t), sl])
            new.append((m_new, acc))
        return tuple(new)

    init = tuple((jnp.full((t, 1), -jnp.inf, F32), jnp.zeros((t, LANES), F32)) for _ in range(2))
    carry = lax.fori_loop(0, i, functools.partial(step, masked=False), init)
    (_, acc0), (_, acc1) = step(i, carry, True)
    out0 = acc0 / acc0[:, HEAD_DIM:HEAD_DIM + 1]
    out1 = acc1 / acc1[:, HEAD_DIM:HEAD_DIM + 1]
    o_ref[...] = jnp.where(lane < HEAD_DIM, out0, pltpu.roll(out1, HEAD_DIM, 1)).astype(o_ref.dtype)


def _fox_attn(q, k, v, *, t):
    b, s, _ = q.shape
    return pl.pallas_call(
        functools.partial(_fox_attn_kernel, t=t),
        grid=(b, N_HEADS // 2, s // t),
        in_specs=[
            pl.BlockSpec((None, t, 2 * LANES), lambda bi, hp, i: (bi, i, hp)),
            pl.BlockSpec((None, s, 2 * LANES), lambda bi, hp, i: (bi, 0, hp)),
            pl.BlockSpec((None, s, 2 * LANES), lambda bi, hp, i: (bi, 0, hp)),
        ],
        out_specs=pl.BlockSpec((None, t, LANES), lambda bi, hp, i: (bi, i, hp)),
        out_shape=jax.ShapeDtypeStruct((b, s, BRANCH_WIDTH), BF16),
        compiler_params=_params("parallel", "parallel", "arbitrary"),
        name="fox_attn",
    )(q, k, v)


def _proj_dsa_kernel(h_ref, gmix_ref, w_ref, cos_ref, sin_ref, gq_ref, gk_ref,
                     dq_ref, iq_ref, k_ref, v_ref, ik_ref, iw_ref, *, tm):
    u = _rms(h_ref[...], gmix_ref[...]).astype(BF16)
    p = _dot(u, w_ref[...])
    qw = N_HEADS * LANES
    lane = lax.broadcasted_iota(I32, (tm, LANES), 1)
    cos = cos_ref[...]
    sin = sin_ref[...]

    def rope(x):
        rot = jnp.where(lane < HEAD_DIM // 2, pltpu.roll(x, LANES - HEAD_DIM // 2, 1), pltpu.roll(x, HEAD_DIM // 2, 1))
        return x * cos + rot * sin

    kv = p[:, 2 * qw:2 * qw + LANES]
    kk = jnp.where(lane < HEAD_DIM, kv, 0.0)
    k_ref[...] = rope(_head_rms(kk, gk_ref[...])).astype(BF16)
    v_ref[...] = jnp.where(lane < HEAD_DIM, pltpu.roll(kv, HEAD_DIM, 1),
                           jnp.where(lane == HEAD_DIM, 1.0, 0.0)).astype(BF16)

    ikw = p[:, 2 * qw + LANES:]
    ik_ref[...] = rope(jnp.where(lane < HEAD_DIM, ikw, 0.0)).astype(BF16)
    iw_ref[...] = ikw

    gq = gq_ref[...]
    iw_scale = N_HEADS ** -0.5 * HEAD_DIM ** -0.5
    for h in range(N_HEADS):
        dq = rope(_head_rms(p[:, h * LANES:(h + 1) * LANES], gq)) * (HEAD_DIM ** -0.5 * LOG2E)
        dq_ref[:, h * LANES:(h + 1) * LANES] = dq.astype(BF16)
        w_h = ikw[:, HEAD_DIM + h:HEAD_DIM + h + 1] * iw_scale
        iq = rope(p[:, qw + h * LANES:qw + (h + 1) * LANES]) * w_h
        iq_ref[:, h * LANES:(h + 1) * LANES] = iq.astype(BF16)


def _proj_dsa(h, gmix, w, cos, sin, gq, gk, *, tm):
    b, s, d = h.shape
    qw = N_HEADS * LANES
    nw = w.shape[1]
    const = lambda bi, i: (0, 0)
    tok = lambda bi, i: (bi, i, 0)
    return pl.pallas_call(
        functools.partial(_proj_dsa_kernel, tm=tm),
        grid=(b, s // tm),
        in_specs=[
            pl.BlockSpec((None, tm, d), tok),
            pl.BlockSpec((1, d), const),
            pl.BlockSpec((d, nw), const),
            pl.BlockSpec((None, tm, LANES), tok),
            pl.BlockSpec((None, tm, LANES), tok),
            pl.BlockSpec((1, LANES), const),
            pl.BlockSpec((1, LANES), const),
        ],
        out_specs=[
            pl.BlockSpec((None, tm, qw), tok),
            pl.BlockSpec((None, tm, qw), tok),
            pl.BlockSpec((None, tm, LANES), tok),
            pl.BlockSpec((None, tm, LANES), tok),
            pl.BlockSpec((None, tm, LANES), tok),
            pl.BlockSpec((None, tm, LANES), tok),
        ],
        out_shape=[
            jax.ShapeDtypeStruct((b, s, qw), BF16),
            jax.ShapeDtypeStruct((b, s, qw), BF16),
            jax.ShapeDtypeStruct((b, s, LANES), BF16),
            jax.ShapeDtypeStruct((b, s, LANES), BF16),
            jax.ShapeDtypeStruct((b, s, LANES), BF16),
            jax.ShapeDtypeStruct((b, s, LANES), F32),
        ],
        compiler_params=_params("parallel", "parallel"),
        name="proj_dsa",
    )(h, gmix, w, cos, sin, gq, gk)


def _dsa_attn_kernel(iq_ref, dq_ref, iw_ref, ik_ref, k_ref, v_ref, o_ref,
                     key_ref, keyq_ref, jsel_ref, m_ref, acc_ref, *, tq, tk, k_sel, idx_bits):
    i = pl.program_id(1)
    npair = ((i + 1) * tq + 2 * tk - 1) // (2 * tk)
    nblk = 2 * npair
    groups = tk // LANES
    qpos = i * tq + lax.broadcasted_iota(I32, (tk, tq), 1)
    krow = lax.broadcasted_iota(I32, (tk, tq), 0)

    def stack_heads(ref):
        return jnp.concatenate([ref[:, h * LANES:(h + 1) * LANES] for h in range(N_HEADS)], axis=0)

    def rows(x):
        return jnp.broadcast_to(x, (tk, tq))

    def transpose_i32(x):
        return lax.bitcast_convert_type(lax.bitcast_convert_type(x, F32).T, I32)

    def per_query_rows(x):
        sq = transpose_i32(jnp.broadcast_to(x, (tq, tq)))
        return jnp.concatenate([sq] * (tk // tq), axis=1)

    iw_t = iw_ref[...].T
    pos_w = [iw_t[HEAD_DIM + h:HEAD_DIM + h + 1, :] > 0.0 for h in range(N_HEADS)]
    lo_b = [rows(jnp.where(pw, 0.0, -jnp.inf)) for pw in pos_w]
    hi_b = [rows(jnp.where(pw, jnp.inf, 0.0)) for pw in pos_w]
    iq = stack_heads(iq_ref)

    def score_pair(cp, _):
        for u in range(2):
            c = 2 * cp + u
            start = pl.multiple_of(c * tk, tk)
            x = _dot_nt(ik_ref[pl.ds(start, tk), :], iq)
            sc = None
            for h in range(N_HEADS):
                t = jnp.minimum(jnp.maximum(x[:, h * tq:(h + 1) * tq], lo_b[h]), hi_b[h])
                sc = t if sc is None else sc + t
            sc = jnp.where(sc == 0.0, 0.0, sc)
            bits = lax.bitcast_convert_type(sc, I32)
            key = bits ^ ((bits >> 31) & 0x7FFFFFFF)
            key = jnp.where(c * tk + krow <= qpos, key, INT_MIN)
            key_ref[c] = key
            keyq_ref[c] = transpose_i32(key)
        return 0

    lax.fori_loop(0, npair, score_pair, 0)

    fold = 64

    def count(preds):
        one = jnp.ones((tk, tq), I32)
        zero = jnp.zeros((tk, tq), I32)

        def body(cp, accs):
            for u in range(2):
                c = 2 * cp + u
                v = key_ref[c]
                new = []
                for a, p in zip(accs, preds):
                    hit = jnp.where(p(v, c), one, zero)
                    for r in range(tk // fold):
                        a = a + hit[r * fold:(r + 1) * fold]
                    new.append(a)
                accs = tuple(new)
            return accs
        accs = lax.fori_loop(0, npair, body, tuple(jnp.zeros((fold, tq), I32) for _ in preds))
        return [jnp.sum(a, axis=0, keepdims=True) for a in accs]

    want = jnp.full((1, tq), k_sel, I32)

    def one_bit(j, prefix):
        trial = prefix + lax.shift_left(jnp.int32(1), 31 - j)
        trial_b = rows(trial)
        n, = count([lambda v, c: v >= trial_b])
        return jnp.where(n >= want, trial, prefix)

    thr = lax.fori_loop(0, 32, one_bit, jnp.full((1, tq), INT_MIN, I32))
    few = thr == INT_MIN
    thr = jnp.maximum(thr, INT_MIN + 1)
    thr_b = rows(thr)
    n_gt, n_eq = count([lambda v, c: v > thr_b, lambda v, c: v == thr_b])

    need = want - n_gt
    jsel_ref[...] = jnp.full((1, tq), 2 ** idx_bits, I32)

    @pl.when(jnp.max(jnp.where(few, 0, n_eq - need)) > 0)
    def _():
        def idx_step(b, pos):
            trial = pos + lax.shift_left(jnp.int32(1), idx_bits - 1 - b)
            trial_b = rows(trial)
            below, = count([lambda kk, c: (kk == thr_b) & (c * tk + krow < trial_b)])
            return jnp.where(below < need, trial, pos)
        jsel_ref[...] = lax.fori_loop(0, idx_bits, idx_step, jnp.zeros((1, tq), I32))

    thr_q = per_query_rows(thr)
    jsel_q = per_query_rows(jsel_ref[...])
    kcol = lax.broadcasted_iota(I32, (tq, tk), 1)
    dq = stack_heads(dq_ref)
    m_ref[...] = jnp.full(m_ref.shape, NEG_BIG, F32)
    acc_ref[...] = jnp.zeros(acc_ref.shape, F32)

    def attend_pair(cp, _):
        m_old = [m_ref[h] for h in range(N_HEADS)]
        m_run = list(m_old)
        pvs = []
        for u in range(2):
            c = 2 * cp + u
            start = pl.multiple_of(c * tk, tk)
            kk = keyq_ref[c]
            sel = (kk > thr_q) | ((kk == thr_q) & (c * tk + kcol <= jsel_q))
            bias = jnp.where(sel, 0.0, NEG_BIG)
            s = _dot_nt(dq, k_ref[pl.ds(start, tk), :])
            ps = []
            for h in range(N_HEADS):
                sh = s[h * tq:(h + 1) * tq] + bias
                m_new = jnp.maximum(m_run[h], jnp.broadcast_to(jnp.max(sh, axis=-1, keepdims=True), (tq, LANES)))
                ps.append(jnp.exp2(sh - jnp.concatenate([m_new] * groups, axis=1)).astype(BF16))
                if u == 1:
                    pvs[h] = pvs[h] * jnp.exp2(m_run[h] - m_new)
                m_run[h] = m_new
            pv = _dot(jnp.concatenate(ps, axis=0), v_ref[pl.ds(start, tk), :])
            if u == 0:
                pvs = [pv[h * tq:(h + 1) * tq] for h in range(N_HEADS)]
            else:
                pvs = [pvs[h] + pv[h * tq:(h + 1) * tq] for h in range(N_HEADS)]
        for h in range(N_HEADS):
            m_ref[h] = m_run[h]
            acc_ref[h] = jnp.exp2(m_old[h] - m_run[h]) * acc_ref[h] + pvs[h]
        return 0

    lax.fori_loop(0, npair, attend_pair, 0)
    lane = lax.broadcasted_iota(I32, (tq, LANES), 1)
    outs = []
    for h in range(N_HEADS):
        acc = acc_ref[h]
        outs.append(acc / acc[:, HEAD_DIM:HEAD_DIM + 1])
    for hp in range(N_HEADS // 2):
        odd = pltpu.roll(outs[2 * hp + 1], HEAD_DIM, 1)
        o_ref[:, hp * LANES:(hp + 1) * LANES] = jnp.where(lane < HEAD_DIM, outs[2 * hp], odd).astype(o_ref.dtype)


def _dsa_attn(iq, dq, iw, ik, k, v, *, tq, tk, k_sel):
    b, s, qw = iq.shape
    idx_bits = max(1, int(np.ceil(np.log2(s))))
    nchunk = s // tk
    assert nchunk % 2 == 0 and tq == LANES and tk % tq == 0
    tok = lambda bi, i: (bi, i, 0)
    seq = lambda bi, i: (bi, 0, 0)
    return pl.pallas_call(
        functools.partial(_dsa_attn_kernel, tq=tq, tk=tk, k_sel=k_sel, idx_bits=idx_bits),
        grid=(b, s // tq),
        in_specs=[
            pl.BlockSpec((None, tq, qw), tok),
            pl.BlockSpec((None, tq, qw), tok),
            pl.BlockSpec((None, tq, LANES), tok),
            pl.BlockSpec((None, s, LANES), seq),
            pl.BlockSpec((None, s, LANES), seq),
            pl.BlockSpec((None, s, LANES), seq),
        ],
        out_specs=pl.BlockSpec((None, tq, BRANCH_WIDTH), tok),
        out_shape=jax.ShapeDtypeStruct((b, s, BRANCH_WIDTH), BF16),
        scratch_shapes=[
            pltpu.VMEM((nchunk, tk, tq), I32),
            pltpu.VMEM((nchunk, tq, tk), I32),
            pltpu.VMEM((1, tq), I32),
            pltpu.VMEM((N_HEADS, tq, LANES), F32),
            pltpu.VMEM((N_HEADS, tq, LANES), F32),
        ],
        compiler_params=_params("parallel", "arbitrary"),
        name="dsa_attn",
    )(iq, dq, iw, ik, k, v)


def _mem_kv_kernel(mem_ref, g_ref, w_ref, gk_ref, km_ref, vm_ref):
    u = _rms(mem_ref[...], g_ref[...]).astype(BF16)
    p = _dot(u, w_ref[...])
    vm_ref[...] = p[:, BRANCH_WIDTH:].astype(BF16)
    for h in range(MEM_HEADS):
        sl = slice(h * MEM_HEAD_DIM, (h + 1) * MEM_HEAD_DIM)
        km_ref[:, sl] = _rms(p[:, sl], gk_ref[...]).astype(BF16)


def _mem_kv(mem, g, w, gk):
    b, m, d = mem.shape
    const = lambda bi: (0, 0)
    return pl.pallas_call(
        _mem_kv_kernel,
        grid=(b,),
        in_specs=[
            pl.BlockSpec((None, m, d), lambda bi: (bi, 0, 0)),
            pl.BlockSpec((1, d), const),
            pl.BlockSpec((d, 2 * BRANCH_WIDTH), const),
            pl.BlockSpec((1, MEM_HEAD_DIM), const),
        ],
        out_specs=[pl.BlockSpec((None, m, BRANCH_WIDTH), lambda bi: (bi, 0, 0))] * 2,
        out_shape=[jax.ShapeDtypeStruct((b, m, BRANCH_WIDTH), BF16)] * 2,
        compiler_params=_params("parallel"),
        name="mem_kv",
    )(mem, g, w, gk)


def _mem_attn_kernel(h_ref, gmix_ref, w_ref, gq_ref, km_ref, vm_ref, o_ref):
    u = _rms(h_ref[...], gmix_ref[...]).astype(BF16)
    q = _dot(u, w_ref[...])
    for h in range(MEM_HEADS):
        sl = slice(h * MEM_HEAD_DIM, (h + 1) * MEM_HEAD_DIM)
        qh = _rms(q[:, sl], gq_ref[...]).astype(BF16)
        s = _dot_nt(qh, km_ref[:, sl]) * (MEM_HEAD_DIM ** -0.5)
        p = jnp.exp(s - jnp.max(s, axis=-1, keepdims=True))
        p = p / jnp.sum(p, axis=-1, keepdims=True)
        o_ref[:, sl] = _dot(p.astype(BF16), vm_ref[:, sl]).astype(o_ref.dtype)


def _mem_attn(h, gmix, w, gq, km, vm, *, tm):
    b, s, d = h.shape
    m = km.shape[1]
    const = lambda bi, i: (0, 0)
    return pl.pallas_call(
        _mem_attn_kernel,
        grid=(b, s // tm),
        in_specs=[
            pl.BlockSpec((None, tm, d), lambda bi, i: (bi, i, 0)),
            pl.BlockSpec((1, d), const),
            pl.BlockSpec((d, BRANCH_WIDTH), const),
            pl.BlockSpec((1, MEM_HEAD_DIM), const),
            pl.BlockSpec((None, m, BRANCH_WIDTH), lambda bi, i: (bi, 0, 0)),
            pl.BlockSpec((None, m, BRANCH_WIDTH), lambda bi, i: (bi, 0, 0)),
        ],
        out_specs=pl.BlockSpec((None, tm, BRANCH_WIDTH), lambda bi, i: (bi, i, 0)),
        out_shape=jax.ShapeDtypeStruct((b, s, BRANCH_WIDTH), BF16),
        compiler_params=_params("parallel", "parallel"),
        name="mem_attn",
    )(h, gmix, w, gq, km, vm)


def _merge_kernel(h_ref, gmix_ref, oa_ref, ob_ref, oc_ref, wg_ref, wbr_ref, wout_ref, o_ref):
    x = h_ref[...]
    u = _rms(x, gmix_ref[...]).astype(BF16)
    d = x.shape[1]
    mixed = jnp.zeros(x.shape, F32)
    for n, br_ref in enumerate((oa_ref, ob_ref, oc_ref)):
        gate = jax.nn.sigmoid(_dot(u, wg_ref[:, n * d:(n + 1) * d]))
        mixed = mixed + gate * _dot(br_ref[...], wbr_ref[n])
    o_ref[...] = x + _dot(mixed.astype(BF16), wout_ref[...])


def _merge(h, gmix, oa, ob, oc, wg, wbr, wout, *, tm):
    n, d = h.shape
    tok = lambda i: (i, 0)
    const = lambda i: (0, 0)
    return pl.pallas_call(
        _merge_kernel,
        grid=(n // tm,),
        in_specs=[
            pl.BlockSpec((tm, d), tok),
            pl.BlockSpec((1, d), const),
            pl.BlockSpec((tm, BRANCH_WIDTH), tok),
            pl.BlockSpec((tm, BRANCH_WIDTH), tok),
            pl.BlockSpec((tm, BRANCH_WIDTH), tok),
            pl.BlockSpec((d, N_BRANCHES * d), const),
            pl.BlockSpec((N_BRANCHES, BRANCH_WIDTH, d), lambda i: (0, 0, 0)),
            pl.BlockSpec((d, d), const),
        ],
        out_specs=pl.BlockSpec((tm, d), tok),
        out_shape=jax.ShapeDtypeStruct((n, d), F32),
        compiler_params=_params("parallel"),
        name="merge",
    )(h, gmix, oa, ob, oc, wg, wbr, wout)


def _pad_heads(w, n_heads, head_dim):
    d = w.shape[0]
    w = w.reshape(d, n_heads, head_dim)
    return jnp.pad(w, ((0, 0), (0, 0), (0, LANES - head_dim))).reshape(d, n_heads * LANES)


def _pad_cols(w, width):
    return jnp.pad(w, ((0, 0), (0, width - w.shape[1])))


def _pad_gain(g):
    return jnp.pad(g, (0, LANES - g.shape[0])).reshape(1, LANES)


def _tile(n, pref):
    t = min(n, pref)
    assert n % t == 0, (n, t)
    return t


def kernel(x, mem, positions, ffn1_norm, ffn1_w_gate, ffn1_w_up, ffn1_w_down, mix_norm, mem_norm, w_in, b_forget,
           fox_q_norm, fox_k_norm, dsa_q_norm, dsa_k_norm, mem_q_norm, mem_k_norm, w_mem_kv, w_branch, w_out,
           ffn2_norm, ffn2_w_gate, ffn2_w_up, ffn2_w_down):
    b, s, d = x.shape
    depth = w_in.shape[0]
    n = b * s
    k_sel = min(TOPK_MAX, s // 4)
    bw = BRANCH_WIDTH

    inv_freq = ROPE_THETA ** (-jnp.arange(0, HEAD_DIM, 2, dtype=F32) / HEAD_DIM)
    ang = positions.astype(F32)[..., None] * inv_freq
    zeros = jnp.zeros((b, s, LANES - HEAD_DIM), F32)
    cos_t = jnp.concatenate([jnp.cos(ang), jnp.cos(ang), zeros], axis=-1)
    sin_t = jnp.concatenate([-jnp.sin(ang), jnp.sin(ang), zeros], axis=-1)

    tm_ffn = _tile(n, 1024)
    tm = _tile(s, 512)
    t_fox = _tile(s, 512)
    tq_dsa = _tile(s, 128)
    tk_dsa = _tile(s, 256)
    assert tk_dsa >= k_sel and tk_dsa % tq_dsa == 0

    h = x.reshape(n, d)
    for l in range(depth):
        h = _ffn(h, ffn1_norm[l].reshape(1, d), ffn1_w_gate[l].astype(BF16), ffn1_w_up[l].astype(BF16),
                 ffn1_w_down[l].astype(BF16), tm=tm_ffn, tf=256)

        w = w_in[l]
        sizes = (bw, bw, bw, N_HEADS, bw, HEAD_DIM, HEAD_DIM, N_HEADS * HEAD_DIM, HEAD_DIM, N_HEADS, bw, N_BRANCHES * d)
        offs = np.concatenate([[0], np.cumsum(sizes)])
        (w_fq, w_fk, w_fv, w_ff, w_dq, w_dk, w_dv, w_iq, w_ik, w_iw, w_mq, w_g) = [
            w[:, int(offs[j]):int(offs[j + 1])] for j in range(len(sizes))]
        w_fox = jnp.concatenate([_pad_heads(w_fq, N_HEADS, HEAD_DIM), _pad_heads(w_fk, N_HEADS, HEAD_DIM),
                                 _pad_heads(w_fv, N_HEADS, HEAD_DIM), _pad_cols(w_ff, LANES)], axis=1).astype(BF16)
        w_dsa = jnp.concatenate([_pad_heads(w_dq, N_HEADS, HEAD_DIM), _pad_heads(w_iq, N_HEADS, HEAD_DIM), w_dk, w_dv,
                                 _pad_cols(jnp.concatenate([w_ik, w_iw], axis=1), LANES)], axis=1).astype(BF16)
        gmix = mix_norm[l].reshape(1, d)
        h3 = h.reshape(b, s, d)

        fq, fk, fv = _proj_fox(h3, gmix, w_fox, _pad_cols(b_forget[l].reshape(1, N_HEADS), LANES),
                               _pad_gain(fox_q_norm[l]), _pad_gain(fox_k_norm[l]), tm=tm)
        o_a = _fox_attn(fq, fk, fv, t=t_fox)

        dq, iq, dk, dv, ik, iw = _proj_dsa(h3, gmix, w_dsa, cos_t, sin_t, _pad_gain(dsa_q_norm[l]),
                                           _pad_gain(dsa_k_norm[l]), tm=tm)
        o_b = _dsa_attn(iq, dq, iw, ik, dk, dv, tq=tq_dsa, tk=tk_dsa, k_sel=k_sel)

        km, vm = _mem_kv(mem, mem_norm[l].reshape(1, d), w_mem_kv[l].astype(BF16), mem_k_norm[l].reshape(1, MEM_HEAD_DIM))
        o_c = _mem_attn(h3, gmix, w_mq.astype(BF16), mem_q_norm[l].reshape(1, MEM_HEAD_DIM), km, vm, tm=tm)

        h = _merge(h, gmix, o_a.reshape(n, bw), o_b.reshape(n, bw), o_c.reshape(n, bw), w_g.astype(BF16),
                   w_branch[l].astype(BF16), w_out[l].astype(BF16), tm=tm)

        h = _ffn(h, ffn2_norm[l].reshape(1, d), ffn2_w_gate[l].astype(BF16), ffn2_w_up[l].astype(BF16),
                 ffn2_w_down[l].astype(BF16), tm=tm_ffn, tf=256)
    return h.reshape(b, s, d)
```

```python
import functools

import jax
import jax.numpy as jnp
import numpy as np
from jax import lax
from jax.experimental import pallas as pl
from jax.experimental.pallas import tpu as pltpu

F32 = jnp.float32
BF16 = jnp.bfloat16
I32 = jnp.int32

D_MODEL = 1024
N_HEADS = 8
HEAD_DIM = 64
MEM_HEADS = 4
MEM_HEAD_DIM = 128
BRANCH_WIDTH = 512
N_BRANCHES = 3
ROPE_THETA = 10000.0
TOPK_MAX = 256
EPS = 1e-6

LANES = 128
SUBLANES = 8
VMEM_LIMIT_BYTES = 56 * 1024 * 1024
INT_MIN = -(2 ** 31)
NEG_BIG = -1e30
LOG2E = 1.4426950408889634


def _params(*sem):
    return pltpu.CompilerParams(dimension_semantics=sem, vmem_limit_bytes=VMEM_LIMIT_BYTES)


def _rms(x, g):
    return x * lax.rsqrt(jnp.mean(x * x, axis=-1, keepdims=True) + EPS) * g


def _head_rms(x, g):
    ms = jnp.sum(x * x, axis=-1, keepdims=True) * (1.0 / HEAD_DIM)
    return x * lax.rsqrt(ms + EPS) * g


def _dot(a, b):
    return jnp.dot(a, b, preferred_element_type=F32)


def _dot_nt(a, b):
    return lax.dot_general(a, b, (((1,), (1,)), ((), ())), preferred_element_type=F32)


def _split3(x):
    a = x.astype(BF16)
    r = x - a.astype(F32)
    b = r.astype(BF16)
    c = (r - b.astype(F32)).astype(BF16)
    return a, b, c


def _ffn_kernel(x_ref, g_ref, wg_ref, wu_ref, wd_ref, o_ref, xn_ref, acc_ref):
    f = pl.program_id(1)

    @pl.when(f == 0)
    def _():
        xn_ref[...] = _rms(x_ref[...], g_ref[...]).astype(BF16)
        acc_ref[...] = jnp.zeros_like(acc_ref)

    xn = xn_ref[...]
    gate = _dot(xn, wg_ref[...])
    up = _dot(xn, wu_ref[...])
    act = gate * jax.nn.sigmoid(gate) * up
    acc_ref[...] += _dot(act.astype(BF16), wd_ref[...])

    @pl.when(f == pl.num_programs(1) - 1)
    def _():
        o_ref[...] = x_ref[...] + 0.5 * acc_ref[...]


def _ffn(x, g, wg, wu, wd, *, tm, tf):
    n, d = x.shape
    d_ff = wg.shape[1]
    return pl.pallas_call(
        _ffn_kernel,
        grid=(n // tm, d_ff // tf),
        in_specs=[
            pl.BlockSpec((tm, d), lambda i, f: (i, 0)),
            pl.BlockSpec((1, d), lambda i, f: (0, 0)),
            pl.BlockSpec((d, tf), lambda i, f: (0, f)),
            pl.BlockSpec((d, tf), lambda i, f: (0, f)),
            pl.BlockSpec((tf, d), lambda i, f: (f, 0)),
        ],
        out_specs=pl.BlockSpec((tm, d), lambda i, f: (i, 0)),
        out_shape=jax.ShapeDtypeStruct((n, d), F32),
        scratch_shapes=[pltpu.VMEM((tm, d), BF16), pltpu.VMEM((tm, d), F32)],
        compiler_params=_params("parallel", "arbitrary"),
        name="ffn",
    )(x, g, wg, wu, wd)


def _proj_fox_kernel(h_ref, gmix_ref, w_ref, bf_ref, gq_ref, gk_ref, q_ref, k_ref, v_ref, cend_ref, carry_ref, *, tm):
    @pl.when(pl.program_id(1) == 0)
    def _():
        carry_ref[...] = jnp.zeros_like(carry_ref)

    u = _rms(h_ref[...], gmix_ref[...]).astype(BF16)
    p = _dot(u, w_ref[...])
    qw = N_HEADS * LANES

    lane = lax.broadcasted_iota(I32, (tm, LANES), 1)
    z = p[:, 3 * qw:] + bf_ref[...]
    logf = -(jnp.maximum(-z, 0.0) + jnp.log1p(jnp.exp(-jnp.abs(z))))
    logf = jnp.where(lane < N_HEADS, logf, 0.0)

    row = lax.broadcasted_iota(I32, (tm, tm), 0)
    col = lax.broadcasted_iota(I32, (tm, tm), 1)
    tri = jnp.where(col <= row, 1.0, 0.0).astype(BF16)
    l1, l2, l3 = _split3(logf)
    c = _dot(tri, l1) + _dot(tri, l2) + _dot(tri, l3) + carry_ref[...]
    carry_ref[...] = c[tm - 1:tm, :]
    cend_ref[...] = jnp.broadcast_to(c[tm - 1:tm, :], cend_ref.shape)
    c = c * LOG2E

    gq = gq_ref[...]
    gk = gk_ref[...]
    for h in range(N_HEADS):
        qn = _head_rms(p[:, h * LANES:(h + 1) * LANES], gq) * (HEAD_DIM ** -0.5 * LOG2E)
        kn = _head_rms(p[:, qw + h * LANES:qw + (h + 1) * LANES], gk)
        c1, c2, c3 = (t.astype(F32) for t in _split3(c[:, h:h + 1]))
        qa = jnp.where(lane < 64, qn, jnp.where(lane == 64, c1, jnp.where(lane == 65, c2, jnp.where(
            lane == 66, c3, jnp.where(lane < 70, 1.0, 0.0)))))
        ka = jnp.where(lane < 64, kn, jnp.where(lane < 67, 1.0, jnp.where(lane == 67, -c1, jnp.where(
            lane == 68, -c2, jnp.where(lane == 69, -c3, 0.0)))))
        q_ref[:, h * LANES:(h + 1) * LANES] = qa.astype(BF16)
        k_ref[:, h * LANES:(h + 1) * LANES] = ka.astype(BF16)
        va = jnp.where(lane == HEAD_DIM, 1.0, p[:, 2 * qw + h * LANES:2 * qw + (h + 1) * LANES])
        v_ref[:, h * LANES:(h + 1) * LANES] = va.astype(BF16)


def _proj_fox(h, gmix, w, bf, gq, gk, *, tm):
    b, s, d = h.shape
    qw = N_HEADS * LANES
    nw = w.shape[1]
    const = lambda bi, i: (0, 0)
    return pl.pallas_call(
        functools.partial(_proj_fox_kernel, tm=tm),
        grid=(b, s // tm),
        in_specs=[
            pl.BlockSpec((None, tm, d), lambda bi, i: (bi, i, 0)),
            pl.BlockSpec((1, d), const),
            pl.BlockSpec((d, nw), const),
            pl.BlockSpec((1, LANES), const),
            pl.BlockSpec((1, LANES), const),
            pl.BlockSpec((1, LANES), const),
        ],
        out_specs=[
            pl.BlockSpec((None, tm, qw), lambda bi, i: (bi, i, 0)),
            pl.BlockSpec((None, tm, qw), lambda bi, i: (bi, i, 0)),
            pl.BlockSpec((None, tm, qw), lambda bi, i: (bi, i, 0)),
            pl.BlockSpec((None, None, SUBLANES, LANES), lambda bi, i: (bi, i, 0, 0)),
        ],
        out_shape=[jax.ShapeDtypeStruct((b, s, qw), BF16)] * 3
        + [jax.ShapeDtypeStruct((b, s // tm, SUBLANES, LANES), F32)],
        scratch_shapes=[pltpu.VMEM((1, LANES), F32)],
        compiler_params=_params("parallel", "arbitrary"),
        name="proj_fox",
    )(h, gmix, w, bf, gq, gk)


FOX_ROWS = 128
FOX_DEAD_MARGIN = 112.0


def _fox_attn_kernel(cend_ref, slack_ref, q_ref, k_ref, v_ref, o_ref, s_ref, p_ref, m_ref, acc_ref, *, t):
    bi, hp, i = pl.program_id(0), pl.program_id(1), pl.program_id(2)
    heads = [slice(hh * LANES, (hh + 1) * LANES) for hh in range(2)]

    def dead(j, n):
        flags = []
        for hh in range(2):
            head = bi * N_HEADS + 2 * hp + hh
            c_first = jnp.where(i > 0, cend_ref[head, jnp.maximum(i - 1, 0)], 0.0)
            flags.append(c_first - cend_ref[head, j] < slack_ref[0])
        return n + jnp.logical_and(flags[0], flags[1]).astype(I32)

    j0 = lax.fori_loop(0, i, dead, jnp.int32(0))
    groups = t // LANES
    row = lax.broadcasted_iota(I32, (FOX_ROWS, t), 0)
    col = lax.broadcasted_iota(I32, (FOX_ROWS, t), 1)

    def logits(j, slot):
        start = pl.multiple_of(j * t, t)
        for hh, sl in enumerate(heads):
            s_ref[slot, hh] = _dot_nt(q_ref[:, sl], k_ref[pl.ds(start, t), sl])

    def softmax_pv(j, slot, masked):
        start = pl.multiple_of(j * t, t)
        for hh, sl in enumerate(heads):
            for r in range(t // FOX_ROWS):
                rs = slice(r * FOX_ROWS, (r + 1) * FOX_ROWS)
                s = s_ref[slot, hh, rs, :]
                if masked:
                    s = jnp.where(col <= row + r * FOX_ROWS, s, -jnp.inf)
                m_old = m_ref[hh, rs, :]
                m_new = jnp.maximum(m_old, jnp.broadcast_to(jnp.max(s, axis=-1, keepdims=True), (FOX_ROWS, LANES)))
                p_ref[hh, rs, :] = jnp.exp2(s - jnp.concatenate([m_new] * groups, axis=1)).astype(BF16)
                acc_ref[hh, rs, :] = jnp.exp2(m_old - m_new) * acc_ref[hh, rs, :]
                m_ref[hh, rs, :] = m_new
            acc_ref[hh] += _dot(p_ref[hh], v_ref[pl.ds(start, t), sl])

    m_ref[...] = jnp.full(m_ref.shape, -jnp.inf, F32)
    acc_ref[...] = jnp.zeros(acc_ref.shape, F32)
    logits(j0, 0)

    def pair(pi, _):
        j = j0 + 2 * pi
        logits(j + 1, 1)
        softmax_pv(j, 0, False)
        logits(j + 2, 0)
        softmax_pv(j + 1, 1, False)
        return 0

    lax.fori_loop(0, (i - j0) // 2, pair, 0)
    odd = lax.rem(i - j0, 2) == 1

    @pl.when(odd)
    def _():
        logits(i, 1)
        softmax_pv(i - 1, 0, False)
        softmax_pv(i, 1, True)

    @pl.when(jnp.logical_not(odd))
    def _():
        softmax_pv(i, 0, True)

    lane = lax.broadcasted_iota(I32, (t, LANES), 1)
    acc0 = acc_ref[0]
    acc1 = acc_ref[1]
    out0 = acc0 / acc0[:, HEAD_DIM:HEAD_DIM + 1]
    out1 = acc1 / acc1[:, HEAD_DIM:HEAD_DIM + 1]
    o_ref[...] = jnp.where(lane < HEAD_DIM, out0, pltpu.roll(out1, HEAD_DIM, 1)).astype(o_ref.dtype)


def _fox_attn(q, k, v, cend, slack, *, t):
    b, s, _ = q.shape
    grid_spec = pltpu.PrefetchScalarGridSpec(
        num_scalar_prefetch=2,
        grid=(b, N_HEADS // 2, s // t),
        in_specs=[
            pl.BlockSpec((None, t, 2 * LANES), lambda bi, hp, i, *_: (bi, i, hp)),
            pl.BlockSpec((None, s, 2 * LANES), lambda bi, hp, i, *_: (bi, 0, hp)),
            pl.BlockSpec((None, s, 2 * LANES), lambda bi, hp, i, *_: (bi, 0, hp)),
        ],
        out_specs=pl.BlockSpec((None, t, LANES), lambda bi, hp, i, *_: (bi, i, hp)),
        scratch_shapes=[
            pltpu.VMEM((2, 2, t, t), F32),
            pltpu.VMEM((2, t, t), BF16),
            pltpu.VMEM((2, t, LANES), F32),
            pltpu.VMEM((2, t, LANES), F32),
        ],
    )
    return pl.pallas_call(
        functools.partial(_fox_attn_kernel, t=t),
        grid_spec=grid_spec,
        out_shape=jax.ShapeDtypeStruct((b, s, BRANCH_WIDTH), BF16),
        compiler_params=_params("parallel", "parallel", "arbitrary"),
        name="fox_attn",
    )(cend, slack, q, k, v)


def _proj_dsa_kernel(h_ref, gmix_ref, w_ref, cos_ref, sin_ref, gq_ref, gk_ref,
                     dq_ref, iq_ref, k_ref, v_ref, ik_ref, iw_ref, *, tm):
    u = _rms(h_ref[...], gmix_ref[...]).astype(BF16)
    p = _dot(u, w_ref[...])
    qw = N_HEADS * LANES
    lane = lax.broadcasted_iota(I32, (tm, LANES), 1)
    cos = cos_ref[...]
    sin = sin_ref[...]

    def rope(x):
        rot = jnp.where(lane < HEAD_DIM // 2, pltpu.roll(x, LANES - HEAD_DIM // 2, 1), pltpu.roll(x, HEAD_DIM // 2, 1))
        return x * cos + rot * sin

    kv = p[:, 2 * qw:2 * qw + LANES]
    kk = jnp.where(lane < HEAD_DIM, kv, 0.0)
    k_ref[...] = rope(_head_rms(kk, gk_ref[...])).astype(BF16)
    v_ref[...] = jnp.where(lane < HEAD_DIM, pltpu.roll(kv, HEAD_DIM, 1),
                           jnp.where(lane == HEAD_DIM, 1.0, 0.0)).astype(BF16)

    ikw = p[:, 2 * qw + LANES:]
    ik_ref[...] = rope(jnp.where(lane < HEAD_DIM, ikw, 0.0)).astype(BF16)
    iw_ref[...] = ikw

    gq = gq_ref[...]
    iw_scale = N_HEADS ** -0.5 * HEAD_DIM ** -0.5
    for h in range(N_HEADS):
        dq = rope(_head_rms(p[:, h * LANES:(h + 1) * LANES], gq)) * (HEAD_DIM ** -0.5 * LOG2E)
        dq_ref[:, h * LANES:(h + 1) * LANES] = dq.astype(BF16)
        w_h = ikw[:, HEAD_DIM + h:HEAD_DIM + h + 1] * iw_scale
        iq = rope(p[:, qw + h * LANES:qw + (h + 1) * LANES]) * w_h
        iq_ref[:, h * LANES:(h + 1) * LANES] = iq.astype(BF16)


def _proj_dsa(h, gmix, w, cos, sin, gq, gk, *, tm):
    b, s, d = h.shape
    qw = N_HEADS * LANES
    nw = w.shape[1]
    const = lambda bi, i: (0, 0)
    tok = lambda bi, i: (bi, i, 0)
    return pl.pallas_call(
        functools.partial(_proj_dsa_kernel, tm=tm),
        grid=(b, s // tm),
        in_specs=[
            pl.BlockSpec((None, tm, d), tok),
            pl.BlockSpec((1, d), const),
            pl.BlockSpec((d, nw), const),
            pl.BlockSpec((None, tm, LANES), tok),
            pl.BlockSpec((None, tm, LANES), tok),
            pl.BlockSpec((1, LANES), const),
            pl.BlockSpec((1, LANES), const),
        ],
        out_specs=[
            pl.BlockSpec((None, tm, qw), tok),
            pl.BlockSpec((None, tm, qw), tok),
            pl.BlockSpec((None, tm, LANES), tok),
            pl.BlockSpec((None, tm, LANES), tok),
            pl.BlockSpec((None, tm, LANES), tok),
            pl.BlockSpec((None, tm, LANES), tok),
        ],
        out_shape=[
            jax.ShapeDtypeStruct((b, s, qw), BF16),
            jax.ShapeDtypeStruct((b, s, qw), BF16),
            jax.ShapeDtypeStruct((b, s, LANES), BF16),
            jax.ShapeDtypeStruct((b, s, LANES), BF16),
            jax.ShapeDtypeStruct((b, s, LANES), BF16),
            jax.ShapeDtypeStruct((b, s, LANES), F32),
        ],
        compiler_params=_params("parallel", "parallel"),
        name="proj_dsa",
    )(h, gmix, w, cos, sin, gq, gk)


def _dsa_attn_kernel(iq_ref, dq_ref, iw_ref, ik_ref, k_ref, v_ref, o_ref,
                     key_ref, keyq_ref, jsel_ref, m_ref, acc_ref, *, tq, tk, k_sel, idx_bits):
    i = pl.program_id(1)
    npair = ((i + 1) * tq + 2 * tk - 1) // (2 * tk)
    nblk = 2 * npair
    groups = tk // LANES
    qpos = i * tq + lax.broadcasted_iota(I32, (tk, tq), 1)
    krow = lax.broadcasted_iota(I32, (tk, tq), 0)

    def stack_heads(ref):
        return jnp.concatenate([ref[:, h * LANES:(h + 1) * LANES] for h in range(N_HEADS)], axis=0)

    def rows(x):
        return jnp.broadcast_to(x, (tk, tq))

    def transpose_i32(x):
        return lax.bitcast_convert_type(lax.bitcast_convert_type(x, F32).T, I32)

    def per_query_rows(x):
        sq = transpose_i32(jnp.broadcast_to(x, (tq, tq)))
        return jnp.concatenate([sq] * (tk // tq), axis=1)

    iw_t = iw_ref[...].T
    pos_w = [iw_t[HEAD_DIM + h:HEAD_DIM + h + 1, :] > 0.0 for h in range(N_HEADS)]
    lo_b = [rows(jnp.where(pw, 0.0, -jnp.inf)) for pw in pos_w]
    hi_b = [rows(jnp.where(pw, jnp.inf, 0.0)) for pw in pos_w]
    iq = stack_heads(iq_ref)

    def score_pair(cp, _):
        for u in range(2):
            c = 2 * cp + u
            start = pl.multiple_of(c * tk, tk)
            x = _dot_nt(ik_ref[pl.ds(start, tk), :], iq)
            sc = None
            for h in range(N_HEADS):
                t = jnp.minimum(jnp.maximum(x[:, h * tq:(h + 1) * tq], lo_b[h]), hi_b[h])
                sc = t if sc is None else sc + t
            sc = jnp.where(sc == 0.0, 0.0, sc)
            bits = lax.bitcast_convert_type(sc, I32)
            key = bits ^ ((bits >> 31) & 0x7FFFFFFF)
            key = jnp.where(c * tk + krow <= qpos, key, INT_MIN)
            key_ref[c] = key
            keyq_ref[c] = transpose_i32(key)
        return 0

    lax.fori_loop(0, npair, score_pair, 0)

    fold = 64

    def count(preds):
        one = jnp.ones((tk, tq), I32)
        zero = jnp.zeros((tk, tq), I32)

        def body(cp, accs):
            for u in range(2):
                c = 2 * cp + u
                v = key_ref[c]
                new = []
                for a, p in zip(accs, preds):
                    hit = jnp.where(p(v, c), one, zero)
                    for r in range(tk // fold):
                        a = a + hit[r * fold:(r + 1) * fold]
                    new.append(a)
                accs = tuple(new)
            return accs
        accs = lax.fori_loop(0, npair, body, tuple(jnp.zeros((fold, tq), I32) for _ in preds))
        return [jnp.sum(a, axis=0, keepdims=True) for a in accs]

    want = jnp.full((1, tq), k_sel, I32)

    def one_bit(j, prefix):
        trial = prefix + lax.shift_left(jnp.int32(1), 31 - j)
        trial_b = rows(trial)
        n, = count([lambda v, c: v >= trial_b])
        return jnp.where(n >= want, trial, prefix)

    thr = lax.fori_loop(0, 32, one_bit, jnp.full((1, tq), INT_MIN, I32))
    few = thr == INT_MIN
    thr = jnp.maximum(thr, INT_MIN + 1)
    thr_b = rows(thr)
    n_gt, n_eq = count([lambda v, c: v > thr_b, lambda v, c: v == thr_b])

    need = want - n_gt
    jsel_ref[...] = jnp.full((1, tq), 2 ** idx_bits, I32)

    @pl.when(jnp.max(jnp.where(few, 0, n_eq - need)) > 0)
    def _():
        def idx_step(b, pos):
            trial = pos + lax.shift_left(jnp.int32(1), idx_bits - 1 - b)
            trial_b = rows(trial)
            below, = count([lambda kk, c: (kk == thr_b) & (c * tk + krow < trial_b)])
            return jnp.where(below < need, trial, pos)
        jsel_ref[...] = lax.fori_loop(0, idx_bits, idx_step, jnp.zeros((1, tq), I32))

    thr_q = per_query_rows(thr)
    jsel_q = per_query_rows(jsel_ref[...])
    kcol = lax.broadcasted_iota(I32, (tq, tk), 1)
    dq = stack_heads(dq_ref)
    m_ref[...] = jnp.full(m_ref.shape, NEG_BIG, F32)
    acc_ref[...] = jnp.zeros(acc_ref.shape, F32)

    def attend_pair(cp, _):
        m_old = [m_ref[h] for h in range(N_HEADS)]
        m_run = list(m_old)
        pvs = []
        for u in range(2):
            c = 2 * cp + u
            start = pl.multiple_of(c * tk, tk)
            kk = keyq_ref[c]
            sel = (kk > thr_q) | ((kk == thr_q) & (c * tk + kcol <= jsel_q))
            bias = jnp.where(sel, 0.0, NEG_BIG)
            s = _dot_nt(dq, k_ref[pl.ds(start, tk), :])
            ps = []
            for h in range(N_HEADS):
                sh = s[h * tq:(h + 1) * tq] + bias
                m_new = jnp.maximum(m_run[h], jnp.broadcast_to(jnp.max(sh, axis=-1, keepdims=True), (tq, LANES)))
                ps.append(jnp.exp2(sh - jnp.concatenate([m_new] * groups, axis=1)).astype(BF16))
                if u == 1:
                    pvs[h] = pvs[h] * jnp.exp2(m_run[h] - m_new)
                m_run[h] = m_new
            pv = _dot(jnp.concatenate(ps, axis=0), v_ref[pl.ds(start, tk), :])
            if u == 0:
                pvs = [pv[h * tq:(h + 1) * tq] for h in range(N_HEADS)]
            else:
                pvs = [pvs[h] + pv[h * tq:(h + 1) * tq] for h in range(N_HEADS)]
        for h in range(N_HEADS):
            m_ref[h] = m_run[h]
            acc_ref[h] = jnp.exp2(m_old[h] - m_run[h]) * acc_ref[h] + pvs[h]
        return 0

    lax.fori_loop(0, npair, attend_pair, 0)
    lane = lax.broadcasted_iota(I32, (tq, LANES), 1)
    outs = []
    for h in range(N_HEADS):
        acc = acc_ref[h]
        outs.append(acc / acc[:, HEAD_DIM:HEAD_DIM + 1])
    for hp in range(N_HEADS // 2):
        odd = pltpu.roll(outs[2 * hp + 1], HEAD_DIM, 1)
        o_ref[:, hp * LANES:(hp + 1) * LANES] = jnp.where(lane < HEAD_DIM, outs[2 * hp], odd).astype(o_ref.dtype)


def _dsa_attn(iq, dq, iw, ik, k, v, *, tq, tk, k_sel):
    b, s, qw = iq.shape
    idx_bits = max(1, int(np.ceil(np.log2(s))))
    nchunk = s // tk
    assert nchunk % 2 == 0 and tq == LANES and tk % tq == 0
    tok = lambda bi, i: (bi, i, 0)
    seq = lambda bi, i: (bi, 0, 0)
    return pl.pallas_call(
        functools.partial(_dsa_attn_kernel, tq=tq, tk=tk, k_sel=k_sel, idx_bits=idx_bits),
        grid=(b, s // tq),
        in_specs=[
            pl.BlockSpec((None, tq, qw), tok),
            pl.BlockSpec((None, tq, qw), tok),
            pl.BlockSpec((None, tq, LANES), tok),
            pl.BlockSpec((None, s, LANES), seq),
            pl.BlockSpec((None, s, LANES), seq),
            pl.BlockSpec((None, s, LANES), seq),
        ],
        out_specs=pl.BlockSpec((None, tq, BRANCH_WIDTH), tok),
        out_shape=jax.ShapeDtypeStruct((b, s, BRANCH_WIDTH), BF16),
        scratch_shapes=[
            pltpu.VMEM((nchunk, tk, tq), I32),
            pltpu.VMEM((nchunk, tq, tk), I32),
            pltpu.VMEM((1, tq), I32),
            pltpu.VMEM((N_HEADS, tq, LANES), F32),
            pltpu.VMEM((N_HEADS, tq, LANES), F32),
        ],
        compiler_params=_params("parallel", "arbitrary"),
        name="dsa_attn",
    )(iq, dq, iw, ik, k, v)


def _mem_kv_kernel(mem_ref, g_ref, w_ref, gk_ref, km_ref, vm_ref):
    u = _rms(mem_ref[...], g_ref[...]).astype(BF16)
    p = _dot(u, w_ref[...])
    vm_ref[...] = p[:, BRANCH_WIDTH:].astype(BF16)
    for h in range(MEM_HEADS):
        sl = slice(h * MEM_HEAD_DIM, (h + 1) * MEM_HEAD_DIM)
        km_ref[:, sl] = _rms(p[:, sl], gk_ref[...]).astype(BF16)


def _mem_kv(mem, g, w, gk):
    b, m, d = mem.shape
    const = lambda bi: (0, 0)
    return pl.pallas_call(
        _mem_kv_kernel,
        grid=(b,),
        in_specs=[
            pl.BlockSpec((None, m, d), lambda bi: (bi, 0, 0)),
            pl.BlockSpec((1, d), const),
            pl.BlockSpec((d, 2 * BRANCH_WIDTH), const),
            pl.BlockSpec((1, MEM_HEAD_DIM), const),
        ],
        out_specs=[pl.BlockSpec((None, m, BRANCH_WIDTH), lambda bi: (bi, 0, 0))] * 2,
        out_shape=[jax.ShapeDtypeStruct((b, m, BRANCH_WIDTH), BF16)] * 2,
        compiler_params=_params("parallel"),
        name="mem_kv",
    )(mem, g, w, gk)


def _mem_attn_kernel(h_ref, gmix_ref, w_ref, gq_ref, km_ref, vm_ref, o_ref):
    u = _rms(h_ref[...], gmix_ref[...]).astype(BF16)
    q = _dot(u, w_ref[...])
    for h in range(MEM_HEADS):
        sl = slice(h * MEM_HEAD_DIM, (h + 1) * MEM_HEAD_DIM)
        qh = _rms(q[:, sl], gq_ref[...]).astype(BF16)
        s = _dot_nt(qh, km_ref[:, sl]) * (MEM_HEAD_DIM ** -0.5)
        p = jnp.exp(s - jnp.max(s, axis=-1, keepdims=True))
        p = p / jnp.sum(p, axis=-1, keepdims=True)
        o_ref[:, sl] = _dot(p.astype(BF16), vm_ref[:, sl]).astype(o_ref.dtype)


def _mem_attn(h, gmix, w, gq, km, vm, *, tm):
    b, s, d = h.shape
    m = km.shape[1]
    const = lambda bi, i: (0, 0)
    return pl.pallas_call(
        _mem_attn_kernel,
        grid=(b, s // tm),
        in_specs=[
            pl.BlockSpec((None, tm, d), lambda bi, i: (bi, i, 0)),
            pl.BlockSpec((1, d), const),
            pl.BlockSpec((d, BRANCH_WIDTH), const),
            pl.BlockSpec((1, MEM_HEAD_DIM), const),
            pl.BlockSpec((None, m, BRANCH_WIDTH), lambda bi, i: (bi, 0, 0)),
            pl.BlockSpec((None, m, BRANCH_WIDTH), lambda bi, i: (bi, 0, 0)),
        ],
        out_specs=pl.BlockSpec((None, tm, BRANCH_WIDTH), lambda bi, i: (bi, i, 0)),
        out_shape=jax.ShapeDtypeStruct((b, s, BRANCH_WIDTH), BF16),
        compiler_params=_params("parallel", "parallel"),
        name="mem_attn",
    )(h, gmix, w, gq, km, vm)


def _merge_kernel(h_ref, gmix_ref, oa_ref, ob_ref, oc_ref, wg_ref, wbr_ref, wout_ref, o_ref):
    x = h_ref[...]
    u = _rms(x, gmix_ref[...]).astype(BF16)
    d = x.shape[1]
    mixed = jnp.zeros(x.shape, F32)
    for n, br_ref in enumerate((oa_ref, ob_ref, oc_ref)):
        gate = jax.nn.sigmoid(_dot(u, wg_ref[:, n * d:(n + 1) * d]))
        mixed = mixed + gate * _dot(br_ref[...], wbr_ref[n])
    o_ref[...] = x + _dot(mixed.astype(BF16), wout_ref[...])


def _merge(h, gmix, oa, ob, oc, wg, wbr, wout, *, tm):
    n, d = h.shape
    tok = lambda i: (i, 0)
    const = lambda i: (0, 0)
    return pl.pallas_call(
        _merge_kernel,
        grid=(n // tm,),
        in_specs=[
            pl.BlockSpec((tm, d), tok),
            pl.BlockSpec((1, d), const),
            pl.BlockSpec((tm, BRANCH_WIDTH), tok),
            pl.BlockSpec((tm, BRANCH_WIDTH), tok),
            pl.BlockSpec((tm, BRANCH_WIDTH), tok),
            pl.BlockSpec((d, N_BRANCHES * d), const),
            pl.BlockSpec((N_BRANCHES, BRANCH_WIDTH, d), lambda i: (0, 0, 0)),
            pl.BlockSpec((d, d), const),
        ],
        out_specs=pl.BlockSpec((tm, d), tok),
        out_shape=jax.ShapeDtypeStruct((n, d), F32),
        compiler_params=_params("parallel"),
        name="merge",
    )(h, gmix, oa, ob, oc, wg, wbr, wout)


def _pad_heads(w, n_heads, head_dim):
    d = w.shape[0]
    w = w.reshape(d, n_heads, head_dim)
    return jnp.pad(w, ((0, 0), (0, 0), (0, LANES - head_dim))).reshape(d, n_heads * LANES)


def _pad_cols(w, width):
    return jnp.pad(w, ((0, 0), (0, width - w.shape[1])))


def _pad_gain(g):
    return jnp.pad(g, (0, LANES - g.shape[0])).reshape(1, LANES)


def _tile(n, pref):
    t = min(n, pref)
    assert n % t == 0, (n, t)
    return t


def kernel(x, mem, positions, ffn1_norm, ffn1_w_gate, ffn1_w_up, ffn1_w_down, mix_norm, mem_norm, w_in, b_forget,
           fox_q_norm, fox_k_norm, dsa_q_norm, dsa_k_norm, mem_q_norm, mem_k_norm, w_mem_kv, w_branch, w_out,
           ffn2_norm, ffn2_w_gate, ffn2_w_up, ffn2_w_down):
    b, s, d = x.shape
    depth = w_in.shape[0]
    n = b * s
    k_sel = min(TOPK_MAX, s // 4)
    bw = BRANCH_WIDTH

    inv_freq = ROPE_THETA ** (-jnp.arange(0, HEAD_DIM, 2, dtype=F32) / HEAD_DIM)
    ang = positions.astype(F32)[..., None] * inv_freq
    zeros = jnp.zeros((b, s, LANES - HEAD_DIM), F32)
    cos_t = jnp.concatenate([jnp.cos(ang), jnp.cos(ang), zeros], axis=-1)
    sin_t = jnp.concatenate([-jnp.sin(ang), jnp.sin(ang), zeros], axis=-1)

    tm_ffn = _tile(n, 1024)
    tm = _tile(s, 512)
    t_fox = _tile(s, 512)
    tq_dsa = _tile(s, 128)
    tk_dsa = _tile(s, 256)
    assert tk_dsa >= k_sel and tk_dsa % tq_dsa == 0

    h = x.reshape(n, d)
    for l in range(depth):
        h = _ffn(h, ffn1_norm[l].reshape(1, d), ffn1_w_gate[l].astype(BF16), ffn1_w_up[l].astype(BF16),
                 ffn1_w_down[l].astype(BF16), tm=tm_ffn, tf=256)

        w = w_in[l]
        sizes = (bw, bw, bw, N_HEADS, bw, HEAD_DIM, HEAD_DIM, N_HEADS * HEAD_DIM, HEAD_DIM, N_HEADS, bw, N_BRANCHES * d)
        offs = np.concatenate([[0], np.cumsum(sizes)])
        (w_fq, w_fk, w_fv, w_ff, w_dq, w_dk, w_dv, w_iq, w_ik, w_iw, w_mq, w_g) = [
            w[:, int(offs[j]):int(offs[j + 1])] for j in range(len(sizes))]
        w_fox = jnp.concatenate([_pad_heads(w_fq, N_HEADS, HEAD_DIM), _pad_heads(w_fk, N_HEADS, HEAD_DIM),
                                 _pad_heads(w_fv, N_HEADS, HEAD_DIM), _pad_cols(w_ff, LANES)], axis=1).astype(BF16)
        w_dsa = jnp.concatenate([_pad_heads(w_dq, N_HEADS, HEAD_DIM), _pad_heads(w_iq, N_HEADS, HEAD_DIM), w_dk, w_dv,
                                 _pad_cols(jnp.concatenate([w_ik, w_iw], axis=1), LANES)], axis=1).astype(BF16)
        gmix = mix_norm[l].reshape(1, d)
        h3 = h.reshape(b, s, d)

        fq, fk, fv, cend = _proj_fox(h3, gmix, w_fox, _pad_cols(b_forget[l].reshape(1, N_HEADS), LANES),
                                     _pad_gain(fox_q_norm[l]), _pad_gain(fox_k_norm[l]), tm=tm)
        cend = cend[:, t_fox // tm - 1::t_fox // tm, 0, :N_HEADS].transpose(0, 2, 1).reshape(b * N_HEADS, s // t_fox)
        qk_max = 1.02 * HEAD_DIM ** 0.5 * jnp.max(jnp.abs(fox_q_norm[l])) * jnp.max(jnp.abs(fox_k_norm[l]))
        slack = -(2.0 * qk_max + FOX_DEAD_MARGIN).reshape(1).astype(F32)
        o_a = _fox_attn(fq, fk, fv, cend, slack, t=t_fox)

        dq, iq, dk, dv, ik, iw = _proj_dsa(h3, gmix, w_dsa, cos_t, sin_t, _pad_gain(dsa_q_norm[l]),
                                           _pad_gain(dsa_k_norm[l]), tm=tm)
        o_b = _dsa_attn(iq, dq, iw, ik, dk, dv, tq=tq_dsa, tk=tk_dsa, k_sel=k_sel)

        km, vm = _mem_kv(mem, mem_norm[l].reshape(1, d), w_mem_kv[l].astype(BF16), mem_k_norm[l].reshape(1, MEM_HEAD_DIM))
        o_c = _mem_attn(h3, gmix, w_mq.astype(BF16), mem_q_norm[l].reshape(1, MEM_HEAD_DIM), km, vm, tm=tm)

        h = _merge(h, gmix, o_a.reshape(n, bw), o_b.reshape(n, bw), o_c.reshape(n, bw), w_g.astype(BF16),
                   w_branch[l].astype(BF16), w_out[l].astype(BF16), tm=tm)

        h = _ffn(h, ffn2_norm[l].reshape(1, d), ffn2_w_gate[l].astype(BF16), ffn2_w_up[l].astype(BF16),
                 ffn2_w_down[l].astype(BF16), tm=tm_ffn, tf=256)
    return h.reshape(b, s, d)
```
